```python
import math
import jax
import jax.numpy as jnp
from jax import lax
import numpy as np

D_MODEL = 2048
BATCH = 8
SEQ = 2048
DEPTH = 1

GRID_W = 64
CTX_LEN = 256
NORM_EPS = 1e-6

RW_HEADS = 16
RW_HEAD = 64
RW_WIDTH = RW_HEADS * RW_HEAD
DECAY_RANK = 96
ICL_RANK = 96
GATE_RANK = 256
RW_GN_EPS = 64e-5

GD_HEADS = 8
GD_HEAD = 128
GD_WIDTH = GD_HEADS * GD_HEAD
SHORT_CONV = 3
CHUNK = 64

D_FF = 5632
FFN_CONV = 3

RW_SPLITS = (RW_WIDTH, 2 * RW_WIDTH, 3 * RW_WIDTH, 3 * RW_WIDTH + 2 * DECAY_RANK, 3 * RW_WIDTH + 2 * DECAY_RANK + 2 * ICL_RANK)
RW_COLS = 3 * RW_WIDTH + 2 * DECAY_RANK + 2 * ICL_RANK + GATE_RANK
GD_SPLITS = (3 * GD_WIDTH, 4 * GD_WIDTH, 4 * GD_WIDTH + 2 * GD_HEADS)
GD_COLS = 4 * GD_WIDTH + 4 * GD_HEADS
GATE_COLS = 2 * D_MODEL
IN_COLS = RW_COLS + GD_COLS + GATE_COLS

kernel_name = 'hybrid_rwkv7_gdn_convglu_prefix_dit'


def rms_norm(x, g, eps=NORM_EPS):
    xf = x.astype(jnp.float32)
    y = xf * lax.rsqrt(jnp.mean(xf * xf, axis=-1, keepdims=True) + eps)
    return (y * g.astype(jnp.float32)).astype(x.dtype)


def l2_normalize(x):
    xf = x.astype(jnp.float32)
    y = xf * lax.rsqrt(jnp.maximum(jnp.sum(xf * xf, axis=-1, keepdims=True), 1e-12))
    return y.astype(x.dtype)


def centred_neighbour_mean(p):
    pp = jnp.pad(p, ((0, 0), (1, 1), (0, 0)))
    return 0.5 * (pp[:, :-2] + pp[:, 2:])


def dwconv1d_centred(x, w):
    K = w.shape[0]
    half = K // 2
    L = x.shape[1]
    xp = jnp.pad(x, ((0, 0), (half, half), (0, 0)))
    acc = xp[:, 0:L] * w[0]
    for i in range(1, K):
        acc = acc + xp[:, i:i + L] * w[i]
    return acc


def dwconv2d_same(x, w):
    C = x.shape[-1]
    return lax.conv_general_dilated(x, w[:, :, None, :], window_strides=(1, 1), padding='SAME',
                                    dimension_numbers=('NHWC', 'HWIO', 'NHWC'), feature_group_count=C)


def rwkv7_prepare(p, mu, k_k, k_a, w0, w_up, a0, a_up, g_up):
    B, L, _ = p.shape
    p = p + mu * (centred_neighbour_mean(p) - p)
    r, k, v, wd, ad, gd = jnp.split(p, RW_SPLITS, axis=-1)
    wd = wd.reshape(B, L, 2, DECAY_RANK)
    ad = ad.reshape(B, L, 2, ICL_RANK)
    wlog = -jax.nn.softplus(-(w0 + jnp.einsum('bldr,drc->bldc', jnp.tanh(wd), w_up))) - 0.5
    decay = jnp.exp(-jnp.exp(wlog.astype(jnp.float32)))
    a = jax.nn.sigmoid(a0 + jnp.einsum('bldr,drc->bldc', ad, a_up))
    g = jnp.dot(jax.nn.sigmoid(gd), g_up)
    heads = lambda t: t.reshape(t.shape[:-1] + (RW_HEADS, RW_HEAD))
    kk = l2_normalize(heads(k * k_k))
    k_dir = k[:, :, None, :] * (1 + (a - 1) * k_a)
    return heads(r), heads(k_dir), heads(v), kk, heads(decay), heads(a), g


def wkv7_args(feats, d):
    r, k_dir, v, kk, decay, a, _ = feats
    return r, decay[:, :, d], k_dir[:, :, d], v, -kk, kk * a[:, :, d]


def wkv7_scan(S0, r, w, k, v, kneg, kka, reverse):
    def step(S, inp):
        r_t, w_t, k_t, v_t, kneg_t, kka_t = inp
        sa = jnp.einsum('bhvk,bhk->bhv', S, kneg_t)
        S = S * w_t[:, :, None, :] + sa[..., None] * kka_t[:, :, None, :] + v_t[..., None] * k_t[:, :, None, :]
        return S, jnp.einsum('bhvk,bhk->bhv', S, r_t)
    xs = tuple(jnp.moveaxis(t.astype(jnp.float32), 1, 0) for t in (r, w, k, v, kneg, kka))
    S, o = lax.scan(step, S0, xs, reverse=reverse)
    return S, jnp.moveaxis(o, 0, 1)


def rwkv7_readout(o, feats, r_k, gn_g, gn_b):
    r, k_dir, v, _, _, _, g = feats
    B, L = r.shape[:2]
    mean = jnp.mean(o, axis=-1, keepdims=True)
    var = jnp.mean(jnp.square(o - mean), axis=-1, keepdims=True)
    on = ((o - mean) * lax.rsqrt(var + RW_GN_EPS)).reshape(B, L, RW_WIDTH)
    on = (on * gn_g + gn_b).astype(g.dtype)
    k_sum = k_dir[:, :, 0] + k_dir[:, :, 1]
    bonus = jnp.sum(r * k_sum * r_k.reshape(RW_HEADS, RW_HEAD), axis=-1, keepdims=True) * v
    return (on + bonus.reshape(B, L, RW_WIDTH)) * g


def rwkv7_mixer(p_ctx, p_lat, mu, k_k, k_a, r_k, w0, w_up, a0, a_up, g_up, gn_g, gn_b):
    f_ctx = rwkv7_prepare(p_ctx, mu, k_k, k_a, w0, w_up, a0, a_up, g_up)
    f_lat = rwkv7_prepare(p_lat, mu, k_k, k_a, w0, w_up, a0, a_up, g_up)
    B = p_lat.shape[0]
    outs_ctx, outs_lat = [], []
    for d in range(2):
        S0 = jnp.zeros((B, RW_HEADS, RW_HEAD, RW_HEAD), jnp.float32)
        S_ctx, o_c = wkv7_scan(S0, *wkv7_args(f_ctx, d), reverse=(d == 1))
        _, o_l = wkv7_scan(S_ctx, *wkv7_args(f_lat, d), reverse=(d == 1))
        outs_ctx.append(o_c)
        outs_lat.append(o_l)
    y_ctx = rwkv7_readout(outs_ctx[0] + outs_ctx[1], f_ctx, r_k, gn_g, gn_b)
    y_lat = rwkv7_readout(outs_lat[0] + outs_lat[1], f_lat, r_k, gn_g, gn_b)
    return y_ctx, y_lat


def gdn_prepare(p, conv_w, a_log, dt_bias):
    B, L, _ = p.shape
    qkv, z, a, b = jnp.split(p, GD_SPLITS, axis=-1)
    qkv = jax.nn.silu(dwconv1d_centred(qkv, conv_w))
    q, k, v = [t.reshape(B, L, GD_HEADS, GD_HEAD) for t in jnp.split(qkv, 3, axis=-1)]
    q = l2_normalize(q) * (GD_HEAD ** -0.5)
    k = l2_normalize(k)
    glog = -jnp.exp(a_log) * jax.nn.softplus(a.reshape(B, L, 2, GD_HEADS) + dt_bias)
    beta = jax.nn.sigmoid(b.reshape(B, L, 2, GD_HEADS))
    return q, k, v, z, glog, beta


def gated_delta_chunked(q, k, v, g, beta, S0, reverse):
    if reverse:
        q, k, v, g, beta = (t[:, ::-1] for t in (q, k, v, g, beta))
    out_dtype = v.dtype
    q, k, v, g, beta = (t.astype(jnp.float32) for t in (q, k, v, g, beta))
    B, L, H, dk = q.shape
    dv = v.shape[-1]
    n = L // CHUNK
    chunks = lambda t: t.reshape(B, n, CHUNK, H, -1).transpose(0, 3, 1, 2, 4)
    qc, kc, vc = chunks(q), chunks(k), chunks(v)
    gc = g.reshape(B, n, CHUNK, H).transpose(0, 3, 1, 2)
    bc = beta.reshape(B, n, CHUNK, H).transpose(0, 3, 1, 2)
    G = jnp.cumsum(gc, axis=-1)
    incl = jnp.tril(jnp.ones((CHUNK, CHUNK), bool))
    strict = jnp.tril(jnp.ones((CHUNK, CHUNK), bool), -1)
    diff = G[..., :, None] - G[..., None, :]
    decay = jnp.where(incl, jnp.exp(jnp.where(incl, diff, 0.0)), 0.0)
    kb = kc * bc[..., None]
    A = jnp.where(strict, jnp.einsum('bhnid,bhnjd->bhnij', kb, kc) * decay, 0.0)
    eye = jnp.eye(CHUNK, dtype=jnp.float32)
    rhs = jnp.concatenate([vc * bc[..., None], kb * jnp.exp(G)[..., None]], axis=-1)
    sol = lax.linalg.triangular_solve(eye + A, rhs, left_side=True, lower=True, unit_diagonal=True)
    u, w = sol[..., :dv], sol[..., dv:]
    attn = jnp.where(incl, jnp.einsum('bhnid,bhnjd->bhnij', qc, kc) * decay, 0.0)

    def step(S, inp):
        q_i, k_i, u_i, w_i, G_i, attn_i = inp
        v_new = u_i - jnp.einsum('bhck,bhkv->bhcv', w_i, S)
        o = jnp.einsum('bhck,bhkv->bhcv', q_i * jnp.exp(G_i)[..., None], S) + jnp.einsum('bhij,bhjv->bhiv', attn_i, v_new)
        G_last = G_i[..., -1]
        k_dec = k_i * jnp.exp(G_last[..., None] - G_i)[..., None]
        S = S * jnp.exp(G_last)[..., None, None] + jnp.einsum('bhck,bhcv->bhkv', k_dec, v_new)
        return S, o

    xs = tuple(jnp.moveaxis(t, 2, 0) for t in (qc, kc, u, w, G, attn))
    S, o = lax.scan(step, S0, xs)
    o = o.transpose(1, 0, 3, 2, 4).reshape(B, L, H, dv).astype(out_dtype)
    if reverse:
        o = o[:, ::-1]
    return S, o


def gdn_readout(o, z, norm_g):
    B, L = z.shape[:2]
    zh = z.reshape(B, L, GD_HEADS, GD_HEAD)
    return (rms_norm(o, norm_g).astype(z.dtype) * jax.nn.silu(zh)).reshape(B, L, GD_WIDTH)


def gdn_mixer(p_ctx, p_lat, conv_w, a_log, dt_bias, norm_g):
    f_ctx = gdn_prepare(p_ctx, conv_w, a_log, dt_bias)
    f_lat = gdn_prepare(p_lat, conv_w, a_log, dt_bias)
    B = p_lat.shape[0]
    outs_ctx, outs_lat = [], []
    for d in range(2):
        S0 = jnp.zeros((B, GD_HEADS, GD_HEAD, GD_HEAD), jnp.float32)
        q, k, v, _, g, beta = f_ctx
        S_ctx, o_c = gated_delta_chunked(q, k, v, g[:, :, d], beta[:, :, d], S0, d == 1)
        q, k, v, _, g, beta = f_lat
        _, o_l = gated_delta_chunked(q, k, v, g[:, :, d], beta[:, :, d], S_ctx, d == 1)
        outs_ctx.append(o_c)
        outs_lat.append(o_l)
    y_ctx = gdn_readout(outs_ctx[0] + outs_ctx[1], f_ctx[3], norm_g)
    y_lat = gdn_readout(outs_lat[0] + outs_lat[1], f_lat[3], norm_g)
    return y_ctx, y_lat


def branch_merge(y_a, y_b, gates, w_a_out, w_b_out, w_o):
    g_a, g_b = jnp.split(gates, 2, axis=-1)
    merged = jax.nn.sigmoid(g_a) * jnp.dot(y_a, w_a_out) + jax.nn.sigmoid(g_b) * jnp.dot(y_b, w_b_out)
    return jnp.dot(merged, w_o)


def conv_ffn(h, n_rows, w1, conv_w, w2):
    B, L, _ = h.shape
    gate, val = jnp.split(jnp.dot(h, w1), 2, axis=-1)
    gate = dwconv2d_same(gate.reshape(B, n_rows, L // n_rows, D_FF), conv_w).reshape(B, L, D_FF)
    return jnp.dot(jax.nn.gelu(gate, approximate=False) * val, w2)


def _fwd_setup_inputs(seed: int = 0) -> dict:
    key = jax.random.key(seed)
    ks = jax.random.split(key, 32)
    f32 = jnp.float32
    nrm = lambda k, shape, s: jax.random.normal(k, shape, f32) * s
    gain = lambda k, shape: 1.0 + 0.02 * jax.random.normal(k, shape, f32)
    L = DEPTH
    dt = jnp.exp(jax.random.uniform(ks[22], (L, 2, GD_HEADS), f32, math.log(1e-3), math.log(1e-1)))
    return {
        'x': nrm(ks[0], (BATCH, SEQ, D_MODEL), 1.0),
        'c': nrm(ks[1], (BATCH, D_MODEL), 1.0),
        'ctx': nrm(ks[2], (BATCH, CTX_LEN, D_MODEL), 1.0),
        'c_ctx': nrm(ks[3], (D_MODEL,), 1.0),
        'w_ada': nrm(ks[4], (L, D_MODEL, 6 * D_MODEL), D_MODEL ** -0.5),
        'b_ada': nrm(ks[5], (L, 6 * D_MODEL), 0.01),
        'norm1_g': gain(ks[6], (L, D_MODEL)),
        'norm2_g': gain(ks[7], (L, D_MODEL)),
        'w_in': nrm(ks[8], (L, D_MODEL, IN_COLS), D_MODEL ** -0.5),
        'rw_mu': jax.random.uniform(ks[9], (L, RW_COLS), f32),
        'rw_k_k': 0.85 + 0.02 * jax.random.normal(ks[10], (L, RW_WIDTH), f32),
        'rw_k_a': gain(ks[11], (L, RW_WIDTH)),
        'rw_r_k': nrm(ks[12], (L, RW_WIDTH), 0.1),
        'rw_w0': jax.random.uniform(ks[13], (L, 2, RW_WIDTH), f32, -6.5, -1.5),
        'rw_w_up': nrm(ks[14], (L, 2, DECAY_RANK, RW_WIDTH), 0.1 * DECAY_RANK ** -0.5),
        'rw_a0': nrm(ks[15], (L, 2, RW_WIDTH), 0.1),
        'rw_a_up': nrm(ks[16], (L, 2, ICL_RANK, RW_WIDTH), 0.3 * ICL_RANK ** -0.5),
        'rw_g_up': nrm(ks[17], (L, GATE_RANK, RW_WIDTH), GATE_RANK ** -0.5),
        'rw_gn_g': gain(ks[18], (L, RW_WIDTH)),
        'rw_gn_b': nrm(ks[19], (L, RW_WIDTH), 0.01),
        'gd_conv_w': nrm(ks[20], (L, SHORT_CONV, 3 * GD_WIDTH), SHORT_CONV ** -0.5),
        'gd_a_log': jnp.log(jax.random.uniform(ks[21], (L, 2, GD_HEADS), f32, 1.0, 16.0)),
        'gd_dt_bias': dt + jnp.log(-jnp.expm1(-dt)),
        'gd_norm_g': gain(ks[23], (L, GD_HEAD)),
        'w_a_out': nrm(ks[24], (L, RW_WIDTH, D_MODEL), RW_WIDTH ** -0.5),
        'w_b_out': nrm(ks[25], (L, GD_WIDTH, D_MODEL), GD_WIDTH ** -0.5),
        'w_o': nrm(ks[26], (L, D_MODEL, D_MODEL), D_MODEL ** -0.5),
        'ffn_w1': nrm(ks[27], (L, D_MODEL, 2 * D_FF), D_MODEL ** -0.5),
        'ffn_conv_w': nrm(ks[28], (L, FFN_CONV, FFN_CONV, D_FF), 1.0 / FFN_CONV),
        'ffn_w2': nrm(ks[29], (L, D_FF, D_MODEL), D_FF ** -0.5),
        'final_norm_g': gain(ks[30], (D_MODEL,)),
    }


def _fwd_reference(x, c, ctx, c_ctx, w_ada, b_ada, norm1_g, norm2_g, w_in, rw_mu, rw_k_k, rw_k_a, rw_r_k,
              rw_w0, rw_w_up, rw_a0, rw_a_up, rw_g_up, rw_gn_g, rw_gn_b, gd_conv_w, gd_a_log, gd_dt_bias,
              gd_norm_g, w_a_out, w_b_out, w_o, ffn_w1, ffn_conv_w, ffn_w2, final_norm_g):
    n_rows = x.shape[1] // GRID_W
    for l in range(DEPTH):
        mod_lat = jnp.dot(jax.nn.silu(c), w_ada[l]) + b_ada[l]
        mod_ctx = jnp.dot(jax.nn.silu(c_ctx), w_ada[l]) + b_ada[l]
        sh1, sc1, g1, sh2, sc2, g2 = jnp.split(mod_lat[:, None, :], 6, axis=-1)
        csh1, csc1, cg1, csh2, csc2, cg2 = jnp.split(mod_ctx[None, None, :], 6, axis=-1)

        h_lat = rms_norm(x, norm1_g[l]) * (1 + sc1) + sh1
        h_ctx = rms_norm(ctx, norm1_g[l]) * (1 + csc1) + csh1
        p_lat = jnp.dot(h_lat, w_in[l])
        p_ctx = jnp.dot(h_ctx, w_in[l])
        rw_lat, gd_lat, gate_lat = jnp.split(p_lat, (RW_COLS, RW_COLS + GD_COLS), axis=-1)
        rw_ctx, gd_ctx, gate_ctx = jnp.split(p_ctx, (RW_COLS, RW_COLS + GD_COLS), axis=-1)
        ya_ctx, ya_lat = rwkv7_mixer(rw_ctx, rw_lat, rw_mu[l], rw_k_k[l], rw_k_a[l], rw_r_k[l], rw_w0[l],
                                     rw_w_up[l], rw_a0[l], rw_a_up[l], rw_g_up[l], rw_gn_g[l], rw_gn_b[l])
        yb_ctx, yb_lat = gdn_mixer(gd_ctx, gd_lat, gd_conv_w[l], gd_a_log[l], gd_dt_bias[l], gd_norm_g[l])
        x = x + g1 * branch_merge(ya_lat, yb_lat, gate_lat, w_a_out[l], w_b_out[l], w_o[l])

        h_lat = rms_norm(x, norm2_g[l]) * (1 + sc2) + sh2
        x = x + g2 * conv_ffn(h_lat, n_rows, ffn_w1[l], ffn_conv_w[l], ffn_w2[l])

        if l < DEPTH - 1:
            ctx = ctx + cg1 * branch_merge(ya_ctx, yb_ctx, gate_ctx, w_a_out[l], w_b_out[l], w_o[l])
            h_ctx = rms_norm(ctx, norm2_g[l]) * (1 + csc2) + csh2
            ctx = ctx + cg2 * conv_ffn(h_ctx, 1, ffn_w1[l], ffn_conv_w[l], ffn_w2[l])
    return rms_norm(x, final_norm_g)


import jax as _jax
import jax.numpy as _jnp

TWIN_FORMAT = 'train_step'
FWD_PARAMS = ['x', 'c', 'ctx', 'c_ctx', 'w_ada', 'b_ada', 'norm1_g', 'norm2_g', 'w_in', 'rw_mu', 'rw_k_k', 'rw_k_a', 'rw_r_k', 'rw_w0', 'rw_w_up', 'rw_a0', 'rw_a_up', 'rw_g_up', 'rw_gn_g', 'rw_gn_b', 'gd_conv_w', 'gd_a_log', 'gd_dt_bias', 'gd_norm_g', 'w_a_out', 'w_b_out', 'w_o', 'ffn_w1', 'ffn_conv_w', 'ffn_w2', 'final_norm_g']
TWIN_WEIGHTS = ['c_ctx', 'w_ada', 'b_ada', 'norm1_g', 'norm2_g', 'w_in', 'rw_mu', 'rw_k_k', 'rw_k_a', 'rw_r_k', 'rw_w0', 'rw_w_up', 'rw_a0', 'rw_a_up', 'rw_g_up', 'rw_gn_g', 'rw_gn_b', 'gd_conv_w', 'gd_a_log', 'gd_dt_bias', 'gd_norm_g', 'w_a_out', 'w_b_out', 'w_o', 'ffn_w1', 'ffn_conv_w', 'ffn_w2', 'final_norm_g']
TWIN_DIFF_INPUT = 'x'
TWIN_INPUTS = ['x', 'c', 'ctx', 'c_ctx', 'w_ada', 'b_ada', 'norm1_g', 'norm2_g', 'w_in', 'rw_mu', 'rw_k_k', 'rw_k_a', 'rw_r_k', 'rw_w0', 'rw_w_up', 'rw_a0', 'rw_a_up', 'rw_g_up', 'rw_gn_g', 'rw_gn_b', 'gd_conv_w', 'gd_a_log', 'gd_dt_bias', 'gd_norm_g', 'w_a_out', 'w_b_out', 'w_o', 'ffn_w1', 'ffn_conv_w', 'ffn_w2', 'final_norm_g', 'loss_target', 'm_c_ctx', 'm_w_ada', 'm_b_ada', 'm_norm1_g', 'm_norm2_g', 'm_w_in', 'm_rw_mu', 'm_rw_k_k', 'm_rw_k_a', 'm_rw_r_k', 'm_rw_w0', 'm_rw_w_up', 'm_rw_a0', 'm_rw_a_up', 'm_rw_g_up', 'm_rw_gn_g', 'm_rw_gn_b', 'm_gd_conv_w', 'm_gd_a_log', 'm_gd_dt_bias', 'm_gd_norm_g', 'm_w_a_out', 'm_w_b_out', 'm_w_o', 'm_ffn_w1', 'm_ffn_conv_w', 'm_ffn_w2', 'm_final_norm_g', 'v_c_ctx', 'v_w_ada', 'v_b_ada', 'v_norm1_g', 'v_norm2_g', 'v_w_in', 'v_rw_mu', 'v_rw_k_k', 'v_rw_k_a', 'v_rw_r_k', 'v_rw_w0', 'v_rw_w_up', 'v_rw_a0', 'v_rw_a_up', 'v_rw_g_up', 'v_rw_gn_g', 'v_rw_gn_b', 'v_gd_conv_w', 'v_gd_a_log', 'v_gd_dt_bias', 'v_gd_norm_g', 'v_w_a_out', 'v_w_b_out', 'v_w_o', 'v_ffn_w1', 'v_ffn_conv_w', 'v_ffn_w2', 'v_final_norm_g']
TWIN_OUTPUTS = ['loss', 'grad_x', 'grad_c_ctx', 'grad_w_ada', 'grad_b_ada', 'grad_norm1_g', 'grad_norm2_g', 'grad_w_in', 'grad_rw_mu', 'grad_rw_k_k', 'grad_rw_k_a', 'grad_rw_r_k', 'grad_rw_w0', 'grad_rw_w_up', 'grad_rw_a0', 'grad_rw_a_up', 'grad_rw_g_up', 'grad_rw_gn_g', 'grad_rw_gn_b', 'grad_gd_conv_w', 'grad_gd_a_log', 'grad_gd_dt_bias', 'grad_gd_norm_g', 'grad_w_a_out', 'grad_w_b_out', 'grad_w_o', 'grad_ffn_w1', 'grad_ffn_conv_w', 'grad_ffn_w2', 'grad_final_norm_g', 'delta_c_ctx', 'delta_w_ada', 'delta_b_ada', 'delta_norm1_g', 'delta_norm2_g', 'delta_w_in', 'delta_rw_mu', 'delta_rw_k_k', 'delta_rw_k_a', 'delta_rw_r_k', 'delta_rw_w0', 'delta_rw_w_up', 'delta_rw_a0', 'delta_rw_a_up', 'delta_rw_g_up', 'delta_rw_gn_g', 'delta_rw_gn_b', 'delta_gd_conv_w', 'delta_gd_a_log', 'delta_gd_dt_bias', 'delta_gd_norm_g', 'delta_w_a_out', 'delta_w_b_out', 'delta_w_o', 'delta_ffn_w1', 'delta_ffn_conv_w', 'delta_ffn_w2', 'delta_final_norm_g', 'new_m_c_ctx', 'new_m_w_ada', 'new_m_b_ada', 'new_m_norm1_g', 'new_m_norm2_g', 'new_m_w_in', 'new_m_rw_mu', 'new_m_rw_k_k', 'new_m_rw_k_a', 'new_m_rw_r_k', 'new_m_rw_w0', 'new_m_rw_w_up', 'new_m_rw_a0', 'new_m_rw_a_up', 'new_m_rw_g_up', 'new_m_rw_gn_g', 'new_m_rw_gn_b', 'new_m_gd_conv_w', 'new_m_gd_a_log', 'new_m_gd_dt_bias', 'new_m_gd_norm_g', 'new_m_w_a_out', 'new_m_w_b_out', 'new_m_w_o', 'new_m_ffn_w1', 'new_m_ffn_conv_w', 'new_m_ffn_w2', 'new_m_final_norm_g', 'new_v_c_ctx', 'new_v_w_ada', 'new_v_b_ada', 'new_v_norm1_g', 'new_v_norm2_g', 'new_v_w_in', 'new_v_rw_mu', 'new_v_rw_k_k', 'new_v_rw_k_a', 'new_v_rw_r_k', 'new_v_rw_w0', 'new_v_rw_w_up', 'new_v_rw_a0', 'new_v_rw_a_up', 'new_v_rw_g_up', 'new_v_rw_gn_g', 'new_v_rw_gn_b', 'new_v_gd_conv_w', 'new_v_gd_a_log', 'new_v_gd_dt_bias', 'new_v_gd_norm_g', 'new_v_w_a_out', 'new_v_w_b_out', 'new_v_w_o', 'new_v_ffn_w1', 'new_v_ffn_conv_w', 'new_v_ffn_w2', 'new_v_final_norm_g']
TWIN_LEAF_KINDS = {'loss': 'loss', 'grad_x': 'grad_x', 'grad_c_ctx': 'grad_w', 'grad_w_ada': 'grad_w', 'grad_b_ada': 'grad_w', 'grad_norm1_g': 'grad_w', 'grad_norm2_g': 'grad_w', 'grad_w_in': 'grad_w', 'grad_rw_mu': 'grad_w', 'grad_rw_k_k': 'grad_w', 'grad_rw_k_a': 'grad_w', 'grad_rw_r_k': 'grad_w', 'grad_rw_w0': 'grad_w', 'grad_rw_w_up': 'grad_w', 'grad_rw_a0': 'grad_w', 'grad_rw_a_up': 'grad_w', 'grad_rw_g_up': 'grad_w', 'grad_rw_gn_g': 'grad_w', 'grad_rw_gn_b': 'grad_w', 'grad_gd_conv_w': 'grad_w', 'grad_gd_a_log': 'grad_w', 'grad_gd_dt_bias': 'grad_w', 'grad_gd_norm_g': 'grad_w', 'grad_w_a_out': 'grad_w', 'grad_w_b_out': 'grad_w', 'grad_w_o': 'grad_w', 'grad_ffn_w1': 'grad_w', 'grad_ffn_conv_w': 'grad_w', 'grad_ffn_w2': 'grad_w', 'grad_final_norm_g': 'grad_w', 'delta_c_ctx': 'delta_w', 'delta_w_ada': 'delta_w', 'delta_b_ada': 'delta_w', 'delta_norm1_g': 'delta_w', 'delta_norm2_g': 'delta_w', 'delta_w_in': 'delta_w', 'delta_rw_mu': 'delta_w', 'delta_rw_k_k': 'delta_w', 'delta_rw_k_a': 'delta_w', 'delta_rw_r_k': 'delta_w', 'delta_rw_w0': 'delta_w', 'delta_rw_w_up': 'delta_w', 'delta_rw_a0': 'delta_w', 'delta_rw_a_up': 'delta_w', 'delta_rw_g_up': 'delta_w', 'delta_rw_gn_g': 'delta_w', 'delta_rw_gn_b': 'delta_w', 'delta_gd_conv_w': 'delta_w', 'delta_gd_a_log': 'delta_w', 'delta_gd_dt_bias': 'delta_w', 'delta_gd_norm_g': 'delta_w', 'delta_w_a_out': 'delta_w', 'delta_w_b_out': 'delta_w', 'delta_w_o': 'delta_w', 'delta_ffn_w1': 'delta_w', 'delta_ffn_conv_w': 'delta_w', 'delta_ffn_w2': 'delta_w', 'delta_final_norm_g': 'delta_w', 'new_m_c_ctx': 'new_m', 'new_m_w_ada': 'new_m', 'new_m_b_ada': 'new_m', 'new_m_norm1_g': 'new_m', 'new_m_norm2_g': 'new_m', 'new_m_w_in': 'new_m', 'new_m_rw_mu': 'new_m', 'new_m_rw_k_k': 'new_m', 'new_m_rw_k_a': 'new_m', 'new_m_rw_r_k': 'new_m', 'new_m_rw_w0': 'new_m', 'new_m_rw_w_up': 'new_m', 'new_m_rw_a0': 'new_m', 'new_m_rw_a_up': 'new_m', 'new_m_rw_g_up': 'new_m', 'new_m_rw_gn_g': 'new_m', 'new_m_rw_gn_b': 'new_m', 'new_m_gd_conv_w': 'new_m', 'new_m_gd_a_log': 'new_m', 'new_m_gd_dt_bias': 'new_m', 'new_m_gd_norm_g': 'new_m', 'new_m_w_a_out': 'new_m', 'new_m_w_b_out': 'new_m', 'new_m_w_o': 'new_m', 'new_m_ffn_w1': 'new_m', 'new_m_ffn_conv_w': 'new_m', 'new_m_ffn_w2': 'new_m', 'new_m_final_norm_g': 'new_m', 'new_v_c_ctx': 'new_v', 'new_v_w_ada': 'new_v', 'new_v_b_ada': 'new_v', 'new_v_norm1_g': 'new_v', 'new_v_norm2_g': 'new_v', 'new_v_w_in': 'new_v', 'new_v_rw_mu': 'new_v', 'new_v_rw_k_k': 'new_v', 'new_v_rw_k_a': 'new_v', 'new_v_rw_r_k': 'new_v', 'new_v_rw_w0': 'new_v', 'new_v_rw_w_up': 'new_v', 'new_v_rw_a0': 'new_v', 'new_v_rw_a_up': 'new_v', 'new_v_rw_g_up': 'new_v', 'new_v_rw_gn_g': 'new_v', 'new_v_rw_gn_b': 'new_v', 'new_v_gd_conv_w': 'new_v', 'new_v_gd_a_log': 'new_v', 'new_v_gd_dt_bias': 'new_v', 'new_v_gd_norm_g': 'new_v', 'new_v_w_a_out': 'new_v', 'new_v_w_b_out': 'new_v', 'new_v_w_o': 'new_v', 'new_v_ffn_w1': 'new_v', 'new_v_ffn_conv_w': 'new_v', 'new_v_ffn_w2': 'new_v', 'new_v_final_norm_g': 'new_v'}


def _forward(args):
    return _fwd_reference(*[args[k] for k in FWD_PARAMS])


def _output_shape():
    out = _jax.eval_shape(lambda: _forward(_fwd_setup_inputs(0)))
    return out.shape, out.dtype

N_MICROBATCH = 1
ADAM_LR = 0.001
ADAM_B1 = 0.9
ADAM_B2 = 0.999
ADAM_EPS = 1e-08
ADAM_WD = 0.01
ADAM_STEP = 10
PER_EXAMPLE_BATCH_AXIS = {'x': 0, 'c': 0, 'ctx': 0, 'loss_target': 0}
SHARED_INPUTS = []
_WEIGHT_DTYPES = {'c_ctx': _jnp.float32, 'w_ada': _jnp.float32, 'b_ada': _jnp.float32, 'norm1_g': _jnp.float32, 'norm2_g': _jnp.float32, 'w_in': _jnp.float32, 'rw_mu': _jnp.float32, 'rw_k_k': _jnp.float32, 'rw_k_a': _jnp.float32, 'rw_r_k': _jnp.float32, 'rw_w0': _jnp.float32, 'rw_w_up': _jnp.float32, 'rw_a0': _jnp.float32, 'rw_a_up': _jnp.float32, 'rw_g_up': _jnp.float32, 'rw_gn_g': _jnp.float32, 'rw_gn_b': _jnp.float32, 'gd_conv_w': _jnp.float32, 'gd_a_log': _jnp.float32, 'gd_dt_bias': _jnp.float32, 'gd_norm_g': _jnp.float32, 'w_a_out': _jnp.float32, 'w_b_out': _jnp.float32, 'w_o': _jnp.float32, 'ffn_w1': _jnp.float32, 'ffn_conv_w': _jnp.float32, 'ffn_w2': _jnp.float32, 'final_norm_g': _jnp.float32}
MOMENT_SCALE = {'c_ctx': 6.779343e-03, 'w_ada': 2.862554e-02, 'b_ada': 4.872180e-02, 'norm1_g': 3.897150e-02, 'norm2_g': 4.839809e-02, 'w_in': 1.699406e-02, 'rw_mu': 2.828888e-02, 'rw_k_k': 1.520656e-01, 'rw_k_a': 1.311689e-01, 'rw_r_k': 1.417570e-01, 'rw_w0': 4.228894e-03, 'rw_w_up': 1.593451e-03, 'rw_a0': 5.345226e-03, 'rw_a_up': 6.194292e-03, 'rw_g_up': 2.159796e-02, 'rw_gn_g': 1.620231e-02, 'rw_gn_b': 1.686122e-02, 'gd_conv_w': 1.831918e-02, 'gd_a_log': 7.374274e-02, 'gd_dt_bias': 7.234837e-02, 'gd_norm_g': 6.774381e-02, 'w_a_out': 1.477070e-02, 'w_b_out': 1.765556e-02, 'w_o': 2.286267e-02, 'ffn_w1': 2.194201e-02, 'ffn_conv_w': 2.207937e-02, 'ffn_w2': 3.615414e-02, 'final_norm_g': 8.155063e+00}


def _to_microbatches(a, axis):
    t = _jnp.moveaxis(a, axis, 0)
    t = t.reshape((N_MICROBATCH, t.shape[0] // N_MICROBATCH) + t.shape[1:])
    return _jnp.moveaxis(t, 1, axis + 1)


def setup_inputs(seed: int = 0) -> dict:
    inp = _fwd_setup_inputs(seed)
    key = _jax.random.fold_in(_jax.random.key(seed), 7919)
    shape, _ = _output_shape()
    out = dict(inp)
    out["loss_target"] = _jax.random.normal(_jax.random.fold_in(key, 0), shape, _jnp.float32)
    for i, name in enumerate(TWIN_WEIGHTS):
        w = inp[name].astype(_jnp.float32)
        if MOMENT_SCALE is None:
            s = _jnp.sqrt(_jnp.mean(_jnp.square(w)) + 1e-30)
        else:
            s = MOMENT_SCALE[name]
        km, kv = _jax.random.split(_jax.random.fold_in(key, i + 1))
        out[name] = w
        out["m_" + name] = s * _jax.random.normal(km, w.shape, _jnp.float32)
        out["v_" + name] = (s * s) * _jax.random.uniform(kv, w.shape, _jnp.float32, 0.5, 1.5)
    if N_MICROBATCH > 1:
        for name, axis in PER_EXAMPLE_BATCH_AXIS.items():
            out[name] = _to_microbatches(out[name], axis)
    return {'x': out['x'], 'c': out['c'], 'ctx': out['ctx'], 'c_ctx': out['c_ctx'], 'w_ada': out['w_ada'], 'b_ada': out['b_ada'], 'norm1_g': out['norm1_g'], 'norm2_g': out['norm2_g'], 'w_in': out['w_in'], 'rw_mu': out['rw_mu'], 'rw_k_k': out['rw_k_k'], 'rw_k_a': out['rw_k_a'], 'rw_r_k': out['rw_r_k'], 'rw_w0': out['rw_w0'], 'rw_w_up': out['rw_w_up'], 'rw_a0': out['rw_a0'], 'rw_a_up': out['rw_a_up'], 'rw_g_up': out['rw_g_up'], 'rw_gn_g': out['rw_gn_g'], 'rw_gn_b': out['rw_gn_b'], 'gd_conv_w': out['gd_conv_w'], 'gd_a_log': out['gd_a_log'], 'gd_dt_bias': out['gd_dt_bias'], 'gd_norm_g': out['gd_norm_g'], 'w_a_out': out['w_a_out'], 'w_b_out': out['w_b_out'], 'w_o': out['w_o'], 'ffn_w1': out['ffn_w1'], 'ffn_conv_w': out['ffn_conv_w'], 'ffn_w2': out['ffn_w2'], 'final_norm_g': out['final_norm_g'], 'loss_target': out['loss_target'], 'm_c_ctx': out['m_c_ctx'], 'm_w_ada': out['m_w_ada'], 'm_b_ada': out['m_b_ada'], 'm_norm1_g': out['m_norm1_g'], 'm_norm2_g': out['m_norm2_g'], 'm_w_in': out['m_w_in'], 'm_rw_mu': out['m_rw_mu'], 'm_rw_k_k': out['m_rw_k_k'], 'm_rw_k_a': out['m_rw_k_a'], 'm_rw_r_k': out['m_rw_r_k'], 'm_rw_w0': out['m_rw_w0'], 'm_rw_w_up': out['m_rw_w_up'], 'm_rw_a0': out['m_rw_a0'], 'm_rw_a_up': out['m_rw_a_up'], 'm_rw_g_up': out['m_rw_g_up'], 'm_rw_gn_g': out['m_rw_gn_g'], 'm_rw_gn_b': out['m_rw_gn_b'], 'm_gd_conv_w': out['m_gd_conv_w'], 'm_gd_a_log': out['m_gd_a_log'], 'm_gd_dt_bias': out['m_gd_dt_bias'], 'm_gd_norm_g': out['m_gd_norm_g'], 'm_w_a_out': out['m_w_a_out'], 'm_w_b_out': out['m_w_b_out'], 'm_w_o': out['m_w_o'], 'm_ffn_w1': out['m_ffn_w1'], 'm_ffn_conv_w': out['m_ffn_conv_w'], 'm_ffn_w2': out['m_ffn_w2'], 'm_final_norm_g': out['m_final_norm_g'], 'v_c_ctx': out['v_c_ctx'], 'v_w_ada': out['v_w_ada'], 'v_b_ada': out['v_b_ada'], 'v_norm1_g': out['v_norm1_g'], 'v_norm2_g': out['v_norm2_g'], 'v_w_in': out['v_w_in'], 'v_rw_mu': out['v_rw_mu'], 'v_rw_k_k': out['v_rw_k_k'], 'v_rw_k_a': out['v_rw_k_a'], 'v_rw_r_k': out['v_rw_r_k'], 'v_rw_w0': out['v_rw_w0'], 'v_rw_w_up': out['v_rw_w_up'], 'v_rw_a0': out['v_rw_a0'], 'v_rw_a_up': out['v_rw_a_up'], 'v_rw_g_up': out['v_rw_g_up'], 'v_rw_gn_g': out['v_rw_gn_g'], 'v_rw_gn_b': out['v_rw_gn_b'], 'v_gd_conv_w': out['v_gd_conv_w'], 'v_gd_a_log': out['v_gd_a_log'], 'v_gd_dt_bias': out['v_gd_dt_bias'], 'v_gd_norm_g': out['v_gd_norm_g'], 'v_w_a_out': out['v_w_a_out'], 'v_w_b_out': out['v_w_b_out'], 'v_w_o': out['v_w_o'], 'v_ffn_w1': out['v_ffn_w1'], 'v_ffn_conv_w': out['v_ffn_conv_w'], 'v_ffn_w2': out['v_ffn_w2'], 'v_final_norm_g': out['v_final_norm_g']}


def _loss(weights, diff, rest, loss_target):
    with _jax.named_scope("forward"):
        args = {**rest, TWIN_DIFF_INPUT: diff, **{k: w.astype(_WEIGHT_DTYPES[k]) for k, w in weights.items()}}
        y = _forward(args)
    with _jax.named_scope("loss_head"):
        err = _jnp.square(y.astype(_jnp.float32) - loss_target)
        return 0.5 * _jnp.sum(_jnp.mean(err, axis=-1)) if err.ndim else 0.5 * err


def _adamw(w, g, m, v):
    m = ADAM_B1 * m + (1.0 - ADAM_B1) * g
    v = ADAM_B2 * v + (1.0 - ADAM_B2) * _jnp.square(g)
    m_hat = m / (1.0 - ADAM_B1 ** ADAM_STEP)
    v_hat = v / (1.0 - ADAM_B2 ** ADAM_STEP)
    delta = -ADAM_LR * (m_hat / (_jnp.sqrt(v_hat) + ADAM_EPS) + ADAM_WD * w)
    return delta, m, v


def reference(x, c, ctx, c_ctx, w_ada, b_ada, norm1_g, norm2_g, w_in, rw_mu, rw_k_k, rw_k_a, rw_r_k, rw_w0, rw_w_up, rw_a0, rw_a_up, rw_g_up, rw_gn_g, rw_gn_b, gd_conv_w, gd_a_log, gd_dt_bias, gd_norm_g, w_a_out, w_b_out, w_o, ffn_w1, ffn_conv_w, ffn_w2, final_norm_g, loss_target, m_c_ctx, m_w_ada, m_b_ada, m_norm1_g, m_norm2_g, m_w_in, m_rw_mu, m_rw_k_k, m_rw_k_a, m_rw_r_k, m_rw_w0, m_rw_w_up, m_rw_a0, m_rw_a_up, m_rw_g_up, m_rw_gn_g, m_rw_gn_b, m_gd_conv_w, m_gd_a_log, m_gd_dt_bias, m_gd_norm_g, m_w_a_out, m_w_b_out, m_w_o, m_ffn_w1, m_ffn_conv_w, m_ffn_w2, m_final_norm_g, v_c_ctx, v_w_ada, v_b_ada, v_norm1_g, v_norm2_g, v_w_in, v_rw_mu, v_rw_k_k, v_rw_k_a, v_rw_r_k, v_rw_w0, v_rw_w_up, v_rw_a0, v_rw_a_up, v_rw_g_up, v_rw_gn_g, v_rw_gn_b, v_gd_conv_w, v_gd_a_log, v_gd_dt_bias, v_gd_norm_g, v_w_a_out, v_w_b_out, v_w_o, v_ffn_w1, v_ffn_conv_w, v_ffn_w2, v_final_norm_g):
    given = dict(x=x, c=c, ctx=ctx, c_ctx=c_ctx, w_ada=w_ada, b_ada=b_ada, norm1_g=norm1_g, norm2_g=norm2_g, w_in=w_in, rw_mu=rw_mu, rw_k_k=rw_k_k, rw_k_a=rw_k_a, rw_r_k=rw_r_k, rw_w0=rw_w0, rw_w_up=rw_w_up, rw_a0=rw_a0, rw_a_up=rw_a_up, rw_g_up=rw_g_up, rw_gn_g=rw_gn_g, rw_gn_b=rw_gn_b, gd_conv_w=gd_conv_w, gd_a_log=gd_a_log, gd_dt_bias=gd_dt_bias, gd_norm_g=gd_norm_g, w_a_out=w_a_out, w_b_out=w_b_out, w_o=w_o, ffn_w1=ffn_w1, ffn_conv_w=ffn_conv_w, ffn_w2=ffn_w2, final_norm_g=final_norm_g, loss_target=loss_target, m_c_ctx=m_c_ctx, m_w_ada=m_w_ada, m_b_ada=m_b_ada, m_norm1_g=m_norm1_g, m_norm2_g=m_norm2_g, m_w_in=m_w_in, m_rw_mu=m_rw_mu, m_rw_k_k=m_rw_k_k, m_rw_k_a=m_rw_k_a, m_rw_r_k=m_rw_r_k, m_rw_w0=m_rw_w0, m_rw_w_up=m_rw_w_up, m_rw_a0=m_rw_a0, m_rw_a_up=m_rw_a_up, m_rw_g_up=m_rw_g_up, m_rw_gn_g=m_rw_gn_g, m_rw_gn_b=m_rw_gn_b, m_gd_conv_w=m_gd_conv_w, m_gd_a_log=m_gd_a_log, m_gd_dt_bias=m_gd_dt_bias, m_gd_norm_g=m_gd_norm_g, m_w_a_out=m_w_a_out, m_w_b_out=m_w_b_out, m_w_o=m_w_o, m_ffn_w1=m_ffn_w1, m_ffn_conv_w=m_ffn_conv_w, m_ffn_w2=m_ffn_w2, m_final_norm_g=m_final_norm_g, v_c_ctx=v_c_ctx, v_w_ada=v_w_ada, v_b_ada=v_b_ada, v_norm1_g=v_norm1_g, v_norm2_g=v_norm2_g, v_w_in=v_w_in, v_rw_mu=v_rw_mu, v_rw_k_k=v_rw_k_k, v_rw_k_a=v_rw_k_a, v_rw_r_k=v_rw_r_k, v_rw_w0=v_rw_w0, v_rw_w_up=v_rw_w_up, v_rw_a0=v_rw_a0, v_rw_a_up=v_rw_a_up, v_rw_g_up=v_rw_g_up, v_rw_gn_g=v_rw_gn_g, v_rw_gn_b=v_rw_gn_b, v_gd_conv_w=v_gd_conv_w, v_gd_a_log=v_gd_a_log, v_gd_dt_bias=v_gd_dt_bias, v_gd_norm_g=v_gd_norm_g, v_w_a_out=v_w_a_out, v_w_b_out=v_w_b_out, v_w_o=v_w_o, v_ffn_w1=v_ffn_w1, v_ffn_conv_w=v_ffn_conv_w, v_ffn_w2=v_ffn_w2, v_final_norm_g=v_final_norm_g)
    weights = {n: given[n] for n in TWIN_WEIGHTS}
    shared = {n: given[n] for n in SHARED_INPUTS}
    per_example = {n: given[n] for n in ['x', 'c', 'ctx']}
    grad_fn = _jax.value_and_grad(_loss, argnums=(0, 1))

    def one_microbatch(ex, loss_target):
        ex = dict(ex)
        diff = ex.pop(TWIN_DIFF_INPUT)
        return grad_fn(weights, diff, {**shared, **ex}, loss_target)

    if N_MICROBATCH == 1:
        loss, (grad_w, grad_x) = one_microbatch(per_example, given["loss_target"])
    else:
        def body(carry, xs):
            loss_sum, grad_sum = carry
            l_k, (gw_k, gx_k) = one_microbatch(xs[0], xs[1])
            with _jax.named_scope("update"):
                return (loss_sum + l_k, _jax.tree.map(_jnp.add, grad_sum, gw_k)), gx_k

        init = (_jnp.zeros((), _jnp.float32), _jax.tree.map(_jnp.zeros_like, weights))
        (loss, grad_w), grad_x = _jax.lax.scan(body, init, (per_example, given["loss_target"]))
    with _jax.named_scope("update"):
        delta_w, new_m, new_v = {}, {}, {}
        for n in TWIN_WEIGHTS:
            delta_w[n], new_m[n], new_v[n] = _adamw(weights[n], grad_w[n], given["m_" + n], given["v_" + n])
    return (loss, grad_x, *[grad_w[n] for n in TWIN_WEIGHTS], *[delta_w[n] for n in TWIN_WEIGHTS],
            *[new_m[n] for n in TWIN_WEIGHTS], *[new_v[n] for n in TWIN_WEIGHTS])
```

```python
import functools
import math

import jax
import jax.numpy as jnp
from jax import lax
from jax.experimental import pallas as pl
from jax.experimental.pallas import tpu as pltpu

F32 = jnp.float32
BF16 = jnp.bfloat16
HI = lax.Precision.HIGHEST
GRAD_WIRE = BF16

D = 2048
NCTX = 256
GRID_W = 64
NORM_EPS = 1e-6
RW_H, RW_N = 16, 64
RW_W = RW_H * RW_N
RANK = 96
GATE_RANK = 256
RW_GN_EPS = 64e-5
GD_H, GD_N = 8, 128
GD_W = GD_H * GD_N
CHUNK = 64
D_FF = 5632
NDEV = 8
PW = 4096
VMEM_LIMIT = 56 * 1024 * 1024

ADAM_LR, ADAM_B1, ADAM_B2, ADAM_EPS, ADAM_WD, ADAM_STEP = 0.001, 0.9, 0.999, 1e-08, 0.01, 10


def _cparams(sem):
    return pltpu.CompilerParams(dimension_semantics=sem, vmem_limit_bytes=VMEM_LIMIT)


def _pick(n, cands):
    for c in cands:
        if n % c == 0:
            return c
    return n


def _mm(a, b, *, ta=False, tb=False, name, out_dtype=F32):
    M = a.shape[1] if ta else a.shape[0]
    K = a.shape[0] if ta else a.shape[1]
    N = b.shape[0] if tb else b.shape[1]
    assert K == (b.shape[1] if tb else b.shape[0])
    tm = _pick(M, (512, 768, 256, 128, 64, 16, 8))
    tn = _pick(N, (512, 256, 128))
    dn = (((0 if ta else 1,), (1 if tb else 0,)), ((), ()))

    def body(a_ref, b_ref, o_ref, a16):
        @pl.when(pl.program_id(1) == 0)
        def _():
            a16[...] = a_ref[...].astype(BF16)

        o_ref[...] = lax.dot_general(a16[...], b_ref[...].astype(BF16), dn,
                                     preferred_element_type=F32).astype(out_dtype)

    a_blk = (K, tm) if ta else (tm, K)
    a_spec = pl.BlockSpec(a_blk, (lambda i, j: (0, i)) if ta else (lambda i, j: (i, 0)))
    b_spec = pl.BlockSpec((tn, K), lambda i, j: (j, 0)) if tb else pl.BlockSpec((K, tn), lambda i, j: (0, j))
    return pl.pallas_call(
        body, grid=(M // tm, N // tn), in_specs=[a_spec, b_spec],
        out_specs=pl.BlockSpec((tm, tn), lambda i, j: (i, j)),
        out_shape=jax.ShapeDtypeStruct((M, N), out_dtype),
        scratch_shapes=[pltpu.VMEM(a_blk, BF16)],
        compiler_params=_cparams(("parallel", "arbitrary")), name=name)(a, b)


def _dense(name):
    @jax.custom_vjp
    def f(a, w, sink):
        return _mm(a, w, name=name + "_fwd")

    def fwd(a, w, sink):
        return _mm(a, w, name=name + "_fwd"), (a, w)

    def bwd(res, dc):
        a, w = res
        return (_mm(dc, w, tb=True, name=name + "_da"), None,
                _mm(a, dc, ta=True, name=name + "_dw", out_dtype=GRAD_WIRE))

    f.defvjp(fwd, bwd)
    return f


def _rowwise(fn, name, *, tm, n_nondiff=0):
    def shapes(rows, params):
        tiles = [jax.ShapeDtypeStruct((tm, r.shape[1]), r.dtype) for r in rows]
        ps = [jax.ShapeDtypeStruct(p.shape, p.dtype) for p in params]
        return tiles, ps, jax.eval_shape(fn, *tiles, *ps)

    def row_spec(width):
        return pl.BlockSpec((tm, width), lambda i: (i, 0))

    def whole_spec(shape):
        return pl.BlockSpec(shape, lambda i: (0,) * len(shape))

    def forward(rows, params):
        n = rows[0].shape[0]
        tiles, ps, outs = shapes(rows, params)
        nin = len(rows) + len(params)

        def body(*refs):
            res = fn(*[r[...] for r in refs[:nin]])
            for o_ref, v in zip(refs[nin:], res):
                o_ref[...] = v.astype(o_ref.dtype)

        return pl.pallas_call(
            body, grid=(n // tm,),
            in_specs=[row_spec(t.shape[1]) for t in tiles] + [whole_spec(p.shape) for p in ps],
            out_specs=[row_spec(o.shape[1]) for o in outs],
            out_shape=[jax.ShapeDtypeStruct((n, o.shape[1]), o.dtype) for o in outs],
            compiler_params=_cparams(("parallel",)), name=name + "_fwd")(*rows, *params)

    def backward(rows, params, cts):
        n = rows[0].shape[0]
        tiles, ps, outs = shapes(rows, params)
        nr, npar, nout = len(rows), len(params), len(outs)
        nd = nr - n_nondiff

        def body(*refs):
            ins = [r[...] for r in refs[:nr + npar]]
            ct = tuple(r[...] for r in refs[nr + npar:nr + npar + nout])
            out_refs = refs[nr + npar + nout:]
            fixed = ins[nd:nr]

            def g(*diff):
                return fn(*diff[:nd], *fixed, *diff[nd:])

            _, vjp = jax.vjp(g, *ins[:nd], *ins[nr:])
            grads = vjp(ct)
            for o_ref, v in zip(out_refs[:nd], grads[:nd]):
                o_ref[...] = v.astype(o_ref.dtype)

            @pl.when(pl.program_id(0) == 0)
            def _():
                for o_ref in out_refs[nd:]:
                    o_ref[...] = jnp.zeros_like(o_ref)

            for o_ref, v in zip(out_refs[nd:], grads[nd:]):
                o_ref[...] += v

        res = pl.pallas_call(
            body, grid=(n // tm,),
            in_specs=[row_spec(t.shape[1]) for t in tiles] + [whole_spec(p.shape) for p in ps]
            + [row_spec(o.shape[1]) for o in outs],
            out_specs=[row_spec(t.shape[1]) for t in tiles[:nd]] + [whole_spec(p.shape) for p in ps],
            out_shape=[jax.ShapeDtypeStruct((n, t.shape[1]), t.dtype) for t in tiles[:nd]]
            + [jax.ShapeDtypeStruct(p.shape, p.dtype) for p in ps],
            compiler_params=_cparams(("arbitrary",)), name=name + "_bwd")(*rows, *params, *cts)
        return tuple(res[:nd]) + (None,) * n_nondiff, tuple(res[nd:])

    @jax.custom_vjp
    def op(rows, params):
        return tuple(forward(rows, params))

    def op_fwd(rows, params):
        return tuple(forward(rows, params)), (rows, params)

    def op_bwd(res, cts):
        return backward(res[0], res[1], cts)

    op.defvjp(op_fwd, op_bwd)
    return op


def _valid(t, off, n, mode):
    s = t + off[0]
    ok = (s >= 0) & (s < n)
    if mode[0] == "seg":
        for b in mode[1]:
            ok = ok & ((t >= b) == (s >= b))
    else:
        col = lax.rem(t, mode[1]) + off[1]
        ok = ok & (col >= 0) & (col < mode[1])
    return ok


def _stencil_offsets(mode):
    if mode[0] == "seg":
        return [(-1, 0), (0, 0), (1, 0)]
    w = mode[1]
    return [(di * w + dj, dj) for di in (-1, 0, 1) for dj in (-1, 0, 1)]


def _stencil(name, mode, *, ncols, tm, tc):
    offs = _stencil_offsets(mode)
    J = len(offs)
    halo = 8 if mode[0] == "seg" else 128

    def x_specs(n):
        nb = n // halo
        r = tm // halo
        return [pl.BlockSpec((halo, tc), lambda i, j: (jnp.maximum(i * r - 1, 0), j)),
                pl.BlockSpec((tm, tc), lambda i, j: (i, j)),
                pl.BlockSpec((halo, tc), lambda i, j: (jnp.minimum((i + 1) * r, nb - 1), j))]

    def fill(buf, prev, cur, nxt):
        buf[0:halo, :] = prev[...]
        buf[halo:halo + tm, :] = cur[...]
        buf[halo + tm:, :] = nxt[...]

    def forward(x, coef, suffix=""):
        n = x.shape[0]

        def body(prev, cur, nxt, c_ref, o_ref, buf):
            fill(buf, prev, cur, nxt)
            t = pl.program_id(0) * tm + lax.broadcasted_iota(jnp.int32, (tm, 1), 0)
            acc = jnp.zeros((tm, tc), F32)
            for j, off in enumerate(offs):
                xs = buf[halo + off[0]:halo + off[0] + tm, :]
                acc = acc + jnp.where(_valid(t, off, n, mode), xs, 0.0) * c_ref[j:j + 1, :]
            o_ref[...] = acc

        return pl.pallas_call(
            body, grid=(n // tm, ncols // tc),
            in_specs=x_specs(n) + [pl.BlockSpec((J, tc), lambda i, j: (0, j))],
            out_specs=pl.BlockSpec((tm, tc), lambda i, j: (i, j)),
            out_shape=jax.ShapeDtypeStruct((n, ncols), F32),
            scratch_shapes=[pltpu.VMEM((tm + 2 * halo, tc), F32)],
            compiler_params=_cparams(("parallel", "parallel")), name=name + suffix)(x, x, x, coef)

    def wgrad(x, dy):
        n = x.shape[0]

        def body(prev, cur, nxt, dy_ref, o_ref, buf):
            fill(buf, prev, cur, nxt)
            i = pl.program_id(1)
            t = i * tm + lax.broadcasted_iota(jnp.int32, (tm, 1), 0)

            @pl.when(i == 0)
            def _():
                o_ref[...] = jnp.zeros_like(o_ref)

            dy = dy_ref[...]
            for j, off in enumerate(offs):
                xs = buf[halo + off[0]:halo + off[0] + tm, :]
                o_ref[j:j + 1, :] += jnp.sum(jnp.where(_valid(t, off, n, mode), xs, 0.0) * dy, axis=0, keepdims=True)

        specs = [pl.BlockSpec(s.block_shape, (lambda f: lambda j, i: f(i, j))(s.index_map)) for s in x_specs(n)]
        return pl.pallas_call(
            body, grid=(ncols // tc, n // tm),
            in_specs=specs + [pl.BlockSpec((tm, tc), lambda j, i: (i, j))],
            out_specs=pl.BlockSpec((J, tc), lambda j, i: (0, j)),
            out_shape=jax.ShapeDtypeStruct((J, ncols), F32),
            scratch_shapes=[pltpu.VMEM((tm + 2 * halo, tc), F32)],
            compiler_params=_cparams(("parallel", "arbitrary")), name=name + "_wgrad")(x, x, x, dy)

    @jax.custom_vjp
    def op(x, coef):
        return forward(x, coef)

    def op_fwd(x, coef):
        return forward(x, coef), (x, coef)

    def op_bwd(res, dy):
        x, coef = res
        dx = forward(dy, coef[::-1], "_adj")
        if x.shape[1] != ncols:
            dx = jnp.pad(dx, ((0, 0), (0, x.shape[1] - ncols)))
        return dx, wgrad(x, dy)

    op.defvjp(op_fwd, op_bwd)
    return op


WKV_TC = 16
WKV_TB = 64


def _chunk_index(d, c, n_ctx_chunks, n_chunks):
    rev = jnp.where(c < n_ctx_chunks, n_ctx_chunks - 1 - c, n_ctx_chunks + n_chunks - 1 - c)
    return jnp.where(d == 0, c, rev)


def _to_feature_major(a):
    n = a.shape[0]
    t = a.reshape(n // WKV_TB, WKV_TB, RW_H // 2, 2, RW_N).transpose(2, 4, 0, 3, 1)
    return t.reshape(RW_W // 2, n // WKV_TB * 128)


def _from_feature_major(t):
    n = t.shape[1] // 128 * WKV_TB
    a = t.reshape(RW_H // 2, RW_N, n // WKV_TB, 2, WKV_TB).transpose(2, 4, 0, 3, 1)
    return a.reshape(n, RW_W)


def _pair_consts():
    lane = lax.broadcasted_iota(jnp.int32, (1, 128), 1)
    first = lax.broadcasted_iota(jnp.int32, (RW_N, 128), 1) < RW_N
    same = (lax.div(lax.broadcasted_iota(jnp.int32, (128, 128), 0), RW_N)
            == lax.div(lax.broadcasted_iota(jnp.int32, (128, 128), 1), RW_N))
    return lane < RW_N, lane >= RW_N, first, same.astype(BF16)


def _seg_reduce(x, row, m0, m1, first):
    s0 = jnp.sum(x * jnp.where(m0, row, 0.0), axis=1, keepdims=True)
    s1 = jnp.sum(x * jnp.where(m1, row, 0.0), axis=1, keepdims=True)
    return jnp.where(first, s0, s1)


def _seg_sum_mxu(x, same, passes):
    acc, rest = None, x
    for i in range(passes):
        piece = rest.astype(BF16)
        part = jnp.dot(piece, same, preferred_element_type=F32)
        acc = part if acc is None else acc + part
        if i + 1 < passes:
            rest = rest - piece.astype(F32)
    return acc


def _wkv_forward(r, vT, kk, w, kd, ka, n_ctx):
    n = r.shape[0]
    nc, ncc = n // WKV_TC, n_ctx // WKV_TC
    per_blk = WKV_TB // WKV_TC

    def body(r_ref, vT_ref, kk_ref, w_ref, kd_ref, ka_ref, oT_ref, st_ref, S_ref):
        d, c = pl.program_id(0), pl.program_id(1)
        ci = _chunk_index(d, c, ncc, nc)

        @pl.when(c == 0)
        def _():
            S_ref[...] = jnp.zeros_like(S_ref)

        @pl.when(lax.rem(c, per_blk) == 0)
        def _():
            oT_ref[...] = jnp.zeros_like(oT_ref)

        m0, m1, first, same = _pair_consts()
        lane_t = lax.rem(lax.broadcasted_iota(jnp.int32, (RW_N, 128), 1), WKV_TB)

        def step(s, carry):
            tl = jnp.where(d == 0, s, WKV_TC - 1 - s)
            sel = lane_t == lax.rem(ci, per_blk) * WKV_TC + tl
            row = pl.ds(tl, 1)
            kn_all, w_all, kd_all, ka_all, r_all = -kk_ref[row, :], w_ref[row, :], kd_ref[row, :], ka_ref[row, :], r_ref[row, :]
            pairs = range(RW_H // 2)
            cs = [slice(p * 128, (p + 1) * 128) for p in pairs]
            rows = [slice(p * RW_N, (p + 1) * RW_N) for p in pairs]
            S = [S_ref[p] for p in pairs]
            for p in pairs:
                st_ref[tl, p] = S[p]
            sab = [_seg_reduce(S[p], kn_all[:, cs[p]], m0, m1, first) for p in pairs]
            vb = [_seg_sum_mxu(jnp.where(sel, vT_ref[rows[p], :], 0.0), same, 2) for p in pairs]
            S = [S[p] * w_all[:, cs[p]] + sab[p] * ka_all[:, cs[p]] + vb[p] * kd_all[:, cs[p]] for p in pairs]
            for p in pairs:
                S_ref[p] = S[p]
            for p in pairs:
                ob = _seg_reduce(S[p], r_all[:, cs[p]], m0, m1, first)
                oT_ref[rows[p], :] = jnp.where(sel, ob, oT_ref[rows[p], :])
            return carry

        lax.fori_loop(0, WKV_TC, step, 0)

    def row(dirn):
        if dirn:
            return pl.BlockSpec((None, WKV_TC, RW_W), lambda d, c: (d, _chunk_index(d, c, ncc, nc), 0))
        return pl.BlockSpec((WKV_TC, RW_W), lambda d, c: (_chunk_index(d, c, ncc, nc), 0))

    colT = pl.BlockSpec((RW_W // 2, 128), lambda d, c: (0, lax.div(_chunk_index(d, c, ncc, nc), per_blk)))
    return pl.pallas_call(
        body, grid=(2, nc),
        in_specs=[row(0), colT, row(0), row(1), row(1), row(1)],
        out_specs=[pl.BlockSpec((None, RW_W // 2, 128), lambda d, c: (d, 0, lax.div(_chunk_index(d, c, ncc, nc), per_blk))),
                   pl.BlockSpec((None, WKV_TC, RW_H // 2, RW_N, 128),
                                lambda d, c: (d, _chunk_index(d, c, ncc, nc), 0, 0, 0))],
        out_shape=[jax.ShapeDtypeStruct((2, RW_W // 2, n // WKV_TB * 128), F32),
                   jax.ShapeDtypeStruct((2, n, RW_H // 2, RW_N, 128), F32)],
        scratch_shapes=[pltpu.VMEM((RW_H // 2, RW_N, 128), F32)],
        compiler_params=_cparams(("arbitrary", "arbitrary")), name="wkv_fwd")(r, vT, kk, w, kd, ka)


def _wkv_backward(r, vT, kk, w, kd, ka, st, doT, n_ctx):
    n = r.shape[0]
    nc, ncc = n // WKV_TC, n_ctx // WKV_TC
    per_blk = WKV_TB // WKV_TC

    def body(r_ref, vT_ref, kk_ref, w_ref, kd_ref, ka_ref, st_ref, doT_ref,
             dr_ref, dw_ref, dkd_ref, dkn_ref, dka_ref, dvT_ref, dS_ref):
        d, c = pl.program_id(0), pl.program_id(1)
        ci = _chunk_index(d, nc - 1 - c, ncc, nc)

        @pl.when(c == 0)
        def _():
            dS_ref[...] = jnp.zeros_like(dS_ref)

        @pl.when(lax.rem(c, per_blk) == 0)
        def _():
            dvT_ref[...] = jnp.zeros_like(dvT_ref)

        m0, m1, first, same = _pair_consts()
        lane_t = lax.rem(lax.broadcasted_iota(jnp.int32, (RW_N, 128), 1), WKV_TB)

        def step(s, carry):
            tl = jnp.where(d == 0, WKV_TC - 1 - s, s)
            sel = lane_t == lax.rem(ci, per_blk) * WKV_TC + tl
            row = pl.ds(tl, 1)
            kn_all, w_all, kd_all, ka_all, r_all = -kk_ref[row, :], w_ref[row, :], kd_ref[row, :], ka_ref[row, :], r_ref[row, :]
            pairs = range(RW_H // 2)
            cs = [slice(p * 128, (p + 1) * 128) for p in pairs]
            rows = [slice(p * RW_N, (p + 1) * RW_N) for p in pairs]
            colsum = lambda a: jnp.sum(a, axis=0, keepdims=True)
            Sp = [st_ref[tl, p] for p in pairs]
            dob = [_seg_sum_mxu(jnp.where(sel, doT_ref[rows[p], :], 0.0), same, 2) for p in pairs]
            dS = [dS_ref[p] + dob[p] * r_all[:, cs[p]] for p in pairs]
            dsab = [_seg_reduce(dS[p], ka_all[:, cs[p]], m0, m1, first) for p in pairs]
            sab = [_seg_reduce(Sp[p], kn_all[:, cs[p]], m0, m1, first) for p in pairs]
            vb = [_seg_sum_mxu(jnp.where(sel, vT_ref[rows[p], :], 0.0), same, 2) for p in pairs]
            St = [Sp[p] * w_all[:, cs[p]] + sab[p] * ka_all[:, cs[p]] + vb[p] * kd_all[:, cs[p]] for p in pairs]
            dr_ref[row, :] = jnp.concatenate([colsum(St[p] * dob[p]) for p in pairs], axis=1)
            dw_ref[row, :] = jnp.concatenate([colsum(dS[p] * Sp[p]) for p in pairs], axis=1)
            dka_ref[row, :] = jnp.concatenate([colsum(dS[p] * sab[p]) for p in pairs], axis=1)
            dkd_ref[row, :] = jnp.concatenate([colsum(dS[p] * vb[p]) for p in pairs], axis=1)
            dkn_ref[row, :] = jnp.concatenate([colsum(Sp[p] * dsab[p]) for p in pairs], axis=1)
            for p in pairs:
                dvb = _seg_reduce(dS[p], kd_all[:, cs[p]], m0, m1, first)
                dvT_ref[rows[p], :] = jnp.where(sel, dvb, dvT_ref[rows[p], :])
                dS_ref[p] = dS[p] * w_all[:, cs[p]] + dsab[p] * kn_all[:, cs[p]]
            return carry

        lax.fori_loop(0, WKV_TC, step, 0)

    def cidx(d, c):
        return _chunk_index(d, nc - 1 - c, ncc, nc)

    def row(dirn):
        if dirn:
            return pl.BlockSpec((None, WKV_TC, RW_W), lambda d, c: (d, cidx(d, c), 0))
        return pl.BlockSpec((WKV_TC, RW_W), lambda d, c: (cidx(d, c), 0))

    colT = pl.BlockSpec((RW_W // 2, 128), lambda d, c: (0, lax.div(cidx(d, c), per_blk)))
    colT_d = pl.BlockSpec((None, RW_W // 2, 128), lambda d, c: (d, 0, lax.div(cidx(d, c), per_blk)))
    st_spec = pl.BlockSpec((None, WKV_TC, RW_H // 2, RW_N, 128), lambda d, c: (d, cidx(d, c), 0, 0, 0))
    rows_out = jax.ShapeDtypeStruct((2, n, RW_W), F32)
    return pl.pallas_call(
        body, grid=(2, nc),
        in_specs=[row(0), colT, row(0), row(1), row(1), row(1), st_spec, colT_d],
        out_specs=[row(1)] * 5 + [colT_d],
        out_shape=[rows_out] * 5 + [jax.ShapeDtypeStruct((2, RW_W // 2, n // WKV_TB * 128), F32)],
        scratch_shapes=[pltpu.VMEM((RW_H // 2, RW_N, 128), F32)],
        compiler_params=_cparams(("arbitrary", "arbitrary")), name="wkv_bwd")(r, vT, kk, w, kd, ka, st, doT)


def _wkv_op(n_ctx):
    def readout(oT):
        return jnp.stack([_from_feature_major(oT[0]), _from_feature_major(oT[1])])

    @jax.custom_vjp
    def op(r, v, kk, w, kd, ka):
        return readout(_wkv_forward(r, _to_feature_major(v), kk, w, kd, ka, n_ctx)[0])

    def fwd(r, v, kk, w, kd, ka):
        vT = _to_feature_major(v)
        oT, st = _wkv_forward(r, vT, kk, w, kd, ka, n_ctx)
        return readout(oT), (r, vT, kk, w, kd, ka, st)

    def bwd(res, do):
        r, vT, kk, w, kd, ka, st = res
        doT = jnp.stack([_to_feature_major(do[0]), _to_feature_major(do[1])])
        dr, dw, dkd, dkn, dka, dvT = _wkv_backward(r, vT, kk, w, kd, ka, st, doT, n_ctx)
        return dr[0] + dr[1], _from_feature_major(dvT[0] + dvT[1]), -(dkn[0] + dkn[1]), dw, dkd, dka

    op.defvjp(fwd, bwd)
    return op


GD_HB = 4


def _dot(a, b, dims=((1,), (0,))):
    return lax.dot_general(a, b, (dims, ((), ())), precision=HI, preferred_element_type=F32)


def _gdn_chunk(S, q, k, v, g, beta, rev):
    R = q.shape[0]
    C = CHUNK
    nh = R // C
    ri = lax.broadcasted_iota(jnp.int32, (R, R), 0)
    cj = lax.broadcasted_iota(jnp.int32, (R, R), 1)
    same = lax.div(ri, C) == lax.div(cj, C)
    lag = (ri - cj) * (1 - 2 * rev.astype(jnp.int32))
    incl = same & (lag >= 0)
    strict = same & (lag > 0)
    eye = (ri == cj).astype(F32)
    gb = jnp.broadcast_to(g, (R, GD_N))
    bb = jnp.broadcast_to(beta, (R, GD_N))
    G = _dot(incl.astype(F32), gb)
    Gc = jnp.concatenate([G] * (R // GD_N), axis=1) if R > GD_N else G[:, :R]
    Grow = _dot(jnp.ones((R, R), F32), eye * Gc)
    decay = jnp.where(incl, jnp.exp(jnp.where(incl, Gc - Grow, 0.0)), 0.0)
    kb = k * bb
    A = jnp.where(strict, _dot(kb, k, ((1,), (1,))) * decay, 0.0)
    Nk = -A
    T = eye + Nk
    for _ in range(int(math.log2(C)) - 1):
        Nk = _dot(Nk, Nk)
        T = T + _dot(T, Nk)
    u = _dot(T, v * bb)
    w = _dot(T, kb * jnp.exp(G))
    attn = jnp.where(incl, _dot(q, k, ((1,), (1,))) * decay, 0.0)
    head = lax.div(lax.broadcasted_iota(jnp.int32, (R, GD_N), 0), C)

    def spread(a):
        return jnp.concatenate([jnp.where(head == h, a, 0.0) for h in range(nh)], axis=1)

    v_new = u - _dot(spread(w), S)
    o = _dot(spread(q * jnp.exp(G)), S) + _dot(attn, v_new)
    Gtot = _dot(same.astype(F32), gb)
    k_dec = k * jnp.exp(Gtot - G)
    Gs = jnp.concatenate([Gtot[h * C:(h + 1) * C] for h in range(nh) for _ in range(GD_N // C)], axis=0)
    S_new = S * jnp.exp(Gs) + _dot(spread(k_dec), v_new, ((0,), (0,)))
    return S_new, o


def _stack_heads(ref):
    return jnp.concatenate([ref[:, h * GD_N:(h + 1) * GD_N] for h in range(GD_HB)], axis=0)


def _unstack_heads(ref, a):
    for h in range(GD_HB):
        ref[:, h * GD_N:(h + 1) * GD_N] = a[h * CHUNK:(h + 1) * CHUNK]


def _gdn_specs(n, n_ctx, back):
    nc, ncc = n // CHUNK, n_ctx // CHUNK

    def ci(d, c):
        return _chunk_index(d, nc - 1 - c if back else c, ncc, nc)

    tok = pl.BlockSpec((CHUNK, GD_HB * GD_N), lambda d, h, c: (ci(d, c), h))
    tok_d = pl.BlockSpec((None, CHUNK, GD_HB * GD_N), lambda d, h, c: (d, ci(d, c), h))
    col = pl.BlockSpec((None, GD_HB, CHUNK, 1), lambda d, h, c: (d, h, ci(d, c), 0))
    st = pl.BlockSpec((None, GD_HB, None, GD_N, GD_N), lambda d, h, c: (d, h, ci(d, c), 0, 0))
    return nc, tok, tok_d, col, st


def _ride(ride, bcast, first, last, refs_in, refs_out, sems):
    if not ride:
        return
    start, wait = _exchange_copies(refs_in, refs_out, [bcast] * len(ride), *sems)
    pl.when(first)(start)
    pl.when(last)(wait)


def _grid_ends(nc):
    d, h, c = pl.program_id(0), pl.program_id(1), pl.program_id(2)
    first = (d == 0) & (h == 0) & (c == 0)
    last = (d == 1) & (h == GD_H // GD_HB - 1) & (c == nc - 1)
    return first, last


def _gdn_forward(q, k, v, g, beta, n_ctx, ride=()):
    n = q.shape[0]
    nc, tok, tok_d, col, st = _gdn_specs(n, n_ctx, False)
    nr = len(ride)

    def body(*refs):
        q_ref, k_ref, v_ref, g_ref, b_ref = refs[:5]
        o_ref, st_ref = refs[5 + nr:7 + nr]
        S_ref = refs[7 + 2 * nr]
        first, last = _grid_ends(nc)
        _ride(ride, True, first, last, refs[5:5 + nr], refs[7 + nr:7 + 2 * nr], refs[8 + 2 * nr:])

        @pl.when(pl.program_id(2) == 0)
        def _():
            S_ref[...] = jnp.zeros_like(S_ref)

        S = S_ref[...]
        st_ref[...] = S.reshape(GD_HB, GD_N, GD_N)
        S_new, o = _gdn_chunk(S, _stack_heads(q_ref), _stack_heads(k_ref), _stack_heads(v_ref),
                              g_ref[...].reshape(GD_HB * CHUNK, 1), b_ref[...].reshape(GD_HB * CHUNK, 1),
                              pl.program_id(0) == 1)
        S_ref[...] = S_new
        _unstack_heads(o_ref, o)

    any_spec = pl.BlockSpec(memory_space=pl.ANY)
    res = pl.pallas_call(
        body, grid=(2, GD_H // GD_HB, nc), in_specs=[tok, tok, tok, col, col] + [any_spec] * nr,
        out_specs=[tok_d, st] + [any_spec] * nr,
        out_shape=[jax.ShapeDtypeStruct((2, n, GD_W), F32), jax.ShapeDtypeStruct((2, GD_H, nc, GD_N, GD_N), F32)]
        + _exchange_out_shapes(ride, [True] * nr),
        scratch_shapes=[pltpu.VMEM((GD_HB * GD_N, GD_N), F32)] + (_exchange_sems(nr) if nr else []),
        compiler_params=_cparams(("arbitrary", "arbitrary", "arbitrary")), name="gdn_fwd")(q, k, v, g, beta, *ride)
    return res[0], res[1], tuple(res[2:])


def _gdn_backward(q, k, v, g, beta, st, do, n_ctx, ride=()):
    n = q.shape[0]
    nc, tok, tok_d, col, st_spec = _gdn_specs(n, n_ctx, True)
    nr = len(ride)

    def body(*refs):
        q_ref, k_ref, v_ref, g_ref, b_ref, st_ref, do_ref = refs[:7]
        dq_ref, dk_ref, dv_ref, dg_ref, db_ref = refs[7 + nr:12 + nr]
        dS_ref = refs[12 + 2 * nr]
        first, last = _grid_ends(nc)
        _ride(ride, False, first, last, refs[7:7 + nr], refs[12 + nr:12 + 2 * nr], refs[13 + 2 * nr:])

        @pl.when(pl.program_id(2) == 0)
        def _():
            dS_ref[...] = jnp.zeros_like(dS_ref)

        rev = pl.program_id(0) == 1
        rows = GD_HB * CHUNK
        _, vjp = jax.vjp(lambda S, q_, k_, v_, g_, b_: _gdn_chunk(S, q_, k_, v_, g_, b_, rev),
                         st_ref[...].reshape(GD_HB * GD_N, GD_N), _stack_heads(q_ref), _stack_heads(k_ref),
                         _stack_heads(v_ref), g_ref[...].reshape(rows, 1), b_ref[...].reshape(rows, 1))
        dS, dq, dk, dv, dg, db = vjp((dS_ref[...], _stack_heads(do_ref)))
        dS_ref[...] = dS
        _unstack_heads(dq_ref, dq)
        _unstack_heads(dk_ref, dk)
        _unstack_heads(dv_ref, dv)
        dg_ref[...] = dg.reshape(GD_HB, CHUNK, 1)
        db_ref[...] = db.reshape(GD_HB, CHUNK, 1)

    tok_out = jax.ShapeDtypeStruct((2, n, GD_W), F32)
    col_out = jax.ShapeDtypeStruct((2, GD_H, n, 1), F32)
    any_spec = pl.BlockSpec(memory_space=pl.ANY)
    res = pl.pallas_call(
        body, grid=(2, GD_H // GD_HB, nc), in_specs=[tok, tok, tok, col, col, st_spec, tok_d] + [any_spec] * nr,
        out_specs=[tok_d, tok_d, tok_d, col, col] + [any_spec] * nr,
        out_shape=[tok_out, tok_out, tok_out, col_out, col_out] + _exchange_out_shapes(ride, [False] * nr),
        scratch_shapes=[pltpu.VMEM((GD_HB * GD_N, GD_N), F32)] + (_exchange_sems(nr) if nr else []),
        compiler_params=_cparams(("arbitrary", "arbitrary", "arbitrary")), name="gdn_bwd")(
            q, k, v, g, beta, st, do, *ride)
    return tuple(res[:5]), tuple(res[5:])


def _gdn_op(n_ctx):
    @jax.custom_vjp
    def op(q, k, v, g, beta, shards, sinks):
        o, _, gathered = _gdn_forward(q, k, v, g, beta, n_ctx, shards)
        return o, gathered, sinks

    def fwd(q, k, v, g, beta, shards, sinks):
        o, st, gathered = _gdn_forward(q, k, v, g, beta, n_ctx, shards)
        return (o, gathered, sinks), (q, k, v, g, beta, st, shards)

    def bwd(res, cts):
        q, k, v, g, beta, st, shards = res
        do, _, dsinks = cts
        (dq, dk, dv, dg, db), received = _gdn_backward(q, k, v, g, beta, st, do, n_ctx, tuple(dsinks))
        return dq[0] + dq[1], dk[0] + dk[1], dv[0] + dv[1], dg, db, tuple(None for _ in shards), received

    op.defvjp(fwd, bwd)
    return op


def _bdot(a, b):
    return jnp.dot(a.astype(BF16), b.astype(BF16), preferred_element_type=F32)


def _rms(x, eps=NORM_EPS):
    return x * lax.rsqrt(jnp.mean(x * x, axis=-1, keepdims=True) + eps)


def _softplus(x):
    return jnp.maximum(x, 0.0) + jnp.log(1.0 + jnp.exp(-jnp.abs(x)))


def _heads(width, seg):
    e = (lax.div(lax.broadcasted_iota(jnp.int32, (width, 128), 0), seg)
         == lax.broadcasted_iota(jnp.int32, (width, 128), 1)).astype(F32)
    et = (lax.div(lax.broadcasted_iota(jnp.int32, (128, width), 1), seg)
          == lax.broadcasted_iota(jnp.int32, (128, width), 0)).astype(F32)
    return e, et


def _head_sum(x, seg):
    e, et = _heads(x.shape[1], seg)
    return _dot(_dot(x, e), et)


def _l2n(x, seg):
    return x * lax.rsqrt(jnp.maximum(_head_sum(x * x, seg), 1e-12))


def _f_norm_mod(x, g, sc, sh):
    return ((_rms(x) * g) * (1.0 + sc) + sh,)


def _f_rw_prep(pm, k_k, k_a, w0, wup, a0, aup, gup):
    k = pm[:, RW_W:2 * RW_W]
    kk = _l2n(k * k_k, RW_N)
    ws, kds, kas = [], [], []
    for d in range(2):
        wd = pm[:, 3 * RW_W + 128 * d:3 * RW_W + 128 * (d + 1)]
        ad = pm[:, 3 * RW_W + 256 + 128 * d:3 * RW_W + 256 + 128 * (d + 1)]
        wlog = -_softplus(-(w0[d:d + 1] + _bdot(jnp.tanh(wd), wup[128 * d:128 * (d + 1)]))) - 0.5
        ws.append(jnp.exp(-jnp.exp(wlog)))
        a = jax.nn.sigmoid(a0[d:d + 1] + _bdot(ad, aup[128 * d:128 * (d + 1)]))
        kds.append(k * (1.0 + (a - 1.0) * k_a))
        kas.append(kk * a)
    g = _bdot(jax.nn.sigmoid(pm[:, 3 * RW_W + 512:3 * RW_W + 512 + GATE_RANK]), gup)
    return kk, ws[0], ws[1], kds[0], kds[1], kas[0], kas[1], g


def _f_rw_read(o0, o1, r, kd0, kd1, v, g, r_k, gn_g, gn_b):
    o = o0 + o1
    c = o - _head_sum(o, RW_N) * (1.0 / RW_N)
    var = _head_sum(c * c, RW_N) * (1.0 / RW_N)
    on = c * lax.rsqrt(var + RW_GN_EPS) * gn_g + gn_b
    bonus = _head_sum(r * (kd0 + kd1) * r_k, RW_N) * v
    return ((on + bonus) * g,)


def _f_gd_prep(cq, ab, alog, dtb):
    qkv = cq * jax.nn.sigmoid(cq)
    q = _l2n(qkv[:, :GD_W], GD_N) * (GD_N ** -0.5)
    k = _l2n(qkv[:, GD_W:2 * GD_W], GD_N)
    v = qkv[:, 2 * GD_W:]
    glog = -jnp.exp(alog) * _softplus(ab + dtb)
    lane = lax.broadcasted_iota(jnp.int32, ab.shape, 1)
    return q, k, v, jnp.where(lane < 2 * GD_H, glog, jax.nn.sigmoid(ab))


def _f_gd_read(o0, o1, z, ng):
    o = o0 + o1
    y = o * lax.rsqrt(_head_sum(o * o, GD_N) * (1.0 / GD_N) + NORM_EPS) * jnp.concatenate([ng] * GD_H, axis=1)
    return (y * (z * jax.nn.sigmoid(z)),)


def _f_merge(za, zb, gates):
    return (jax.nn.sigmoid(gates[:, :D]) * za + jax.nn.sigmoid(gates[:, D:]) * zb,)


def _f_res_norm(x, att, g1, ng, sc, sh):
    x1 = x + g1 * att
    return x1, (_rms(x1) * ng) * (1.0 + sc) + sh


def _f_glu(gc, val):
    return (0.5 * gc * (1.0 + lax.erf(gc * (2.0 ** -0.5))) * val,)


def _f_final(x1, ff, target, g2, fg):
    y = _rms(x1 + g2 * ff) * fg
    err = y - target
    return (jnp.broadcast_to(0.5 * jnp.mean(err * err, axis=-1, keepdims=True), (x1.shape[0], 128)),)


def _exchange_copies(ins, outs, bcast, send_sems, recv_sems, local_sems):
    n = len(ins)
    x, y, c = lax.axis_index("x"), lax.axis_index("y"), lax.axis_index("c")
    me = 4 * x + 2 * y + c
    own = [pltpu.make_async_copy(ins[i] if bcast[i] else ins[i].at[me], outs[i].at[me], local_sems.at[i])
           for i in range(n)]
    sends, landings = [], []
    for k in range(1, NDEV):
        kx, ky, kc = (k >> 2) & 1, (k >> 1) & 1, k & 1
        px, py, pc = (1 - x if kx else x), (1 - y if ky else y), (1 - c if kc else c)
        peer = 4 * px + 2 * py + pc
        for i in range(n):
            sem = i * (NDEV - 1) + k - 1

            def copy(dst_block, i=i, sem=sem, peer=peer, dev=(px, py, pc)):
                return pltpu.make_async_remote_copy(
                    src_ref=ins[i] if bcast[i] else ins[i].at[peer], dst_ref=outs[i].at[dst_block],
                    send_sem=send_sems.at[sem], recv_sem=recv_sems.at[sem],
                    device_id=dev, device_id_type=pl.DeviceIdType.MESH)

            sends.append(copy(me))
            landings.append(copy(peer))

    def start():
        for cp in own + sends:
            cp.start()

    def wait():
        for cp, landing in zip(sends, landings):
            cp.wait_send()
            landing.wait_recv()
        for cp in own:
            cp.wait()

    return start, wait


def _exchange_sems(n):
    return [pltpu.SemaphoreType.DMA((n * (NDEV - 1),)), pltpu.SemaphoreType.DMA((n * (NDEV - 1),)),
            pltpu.SemaphoreType.DMA((n,))]


def _exchange_out_shapes(arrays, bcast):
    return [jax.ShapeDtypeStruct((NDEV,) + (a.shape if b else a.shape[1:]), a.dtype) for a, b in zip(arrays, bcast)]


def _exchange(arrays, bcast, name):
    n = len(arrays)

    def body(*refs):
        start, wait = _exchange_copies(refs[:n], refs[n:2 * n], bcast, *refs[2 * n:])
        start()
        wait()

    any_spec = pl.BlockSpec(memory_space=pl.ANY)
    return pl.pallas_call(
        body, in_specs=[any_spec] * n, out_specs=[any_spec] * n, out_shape=_exchange_out_shapes(arrays, bcast),
        scratch_shapes=_exchange_sems(n),
        compiler_params=pltpu.CompilerParams(has_side_effects=True), name=name)(*arrays)


def _adamw(slabs, w, m, v, name):
    R, C = w.shape
    ns = slabs.shape[0]
    tr = _pick(R, (256, 128, 64, 32, 16, 8))

    def body(s_ref, w_ref, m_ref, v_ref, g_ref, d_ref, nm_ref, nv_ref):
        g = s_ref[0].astype(F32)
        for i in range(1, ns):
            g = g + s_ref[i].astype(F32)
        nm = ADAM_B1 * m_ref[...] + (1.0 - ADAM_B1) * g
        nv = ADAM_B2 * v_ref[...] + (1.0 - ADAM_B2) * (g * g)
        m_hat = nm / (1.0 - ADAM_B1 ** ADAM_STEP)
        v_hat = nv / (1.0 - ADAM_B2 ** ADAM_STEP)
        g_ref[...] = g
        d_ref[...] = -ADAM_LR * (m_hat / (jnp.sqrt(v_hat) + ADAM_EPS) + ADAM_WD * w_ref[...])
        nm_ref[...] = nm
        nv_ref[...] = nv

    blk = pl.BlockSpec((tr, C), lambda i: (i, 0))
    out = jax.ShapeDtypeStruct((R, C), F32)
    return pl.pallas_call(
        body, grid=(R // tr,), in_specs=[pl.BlockSpec((ns, tr, C), lambda i: (0, i, 0)), blk, blk, blk],
        out_specs=[blk] * 4, out_shape=[out] * 4, compiler_params=_cparams(("parallel",)), name=name)(slabs, w, m, v)


def _silu(z):
    return z * jax.nn.sigmoid(z)


def _ada_forward(c_ext, w, b):
    n = w.shape[1]
    tn = _pick(n, (512, 256, 128))

    def body(c_ref, w_ref, b_ref, o_ref):
        o_ref[...] = _bdot(_silu(c_ref[...]), w_ref[...]) + b_ref[...]

    return pl.pallas_call(
        body, grid=(n // tn,),
        in_specs=[pl.BlockSpec(c_ext.shape, lambda j: (0, 0)), pl.BlockSpec((D, tn), lambda j: (0, j)),
                  pl.BlockSpec((1, tn), lambda j: (0, j))],
        out_specs=pl.BlockSpec((16, tn), lambda j: (0, j)), out_shape=jax.ShapeDtypeStruct((16, n), F32),
        compiler_params=_cparams(("parallel",)), name="ada_fwd")(c_ext, w, b)


def _ada_dmod(lat, ctxr):
    n = lat.shape[1]
    tn = 2048

    def body(lat_ref, ctx_ref, o_ref):
        o_ref[...] = jnp.concatenate([lat_ref[...], jnp.broadcast_to(jnp.sum(ctx_ref[...], axis=0, keepdims=True), (8, tn))], axis=0)

    blk = pl.BlockSpec((NDEV, tn), lambda j: (0, j))
    return pl.pallas_call(
        body, grid=(n // tn,), in_specs=[blk, blk],
        out_specs=pl.BlockSpec((16, tn), lambda j: (0, j)), out_shape=jax.ShapeDtypeStruct((16, n), F32),
        compiler_params=_cparams(("parallel",)), name="ada_dmod")(lat, ctxr)


def _f_colsum16(dm):
    return (jnp.broadcast_to(jnp.sum(dm, axis=0, keepdims=True), dm.shape),)


def _f_silu_grad(dpart, c_ext):
    sg = jax.nn.sigmoid(c_ext)
    return (dpart * sg * (1.0 + c_ext * (1.0 - sg)),)


def _f_silu(c_ext):
    return (_silu(c_ext),)


def _total_forward(rows):
    n = rows.shape[0]
    tm = _pick(n, (256, 128, 64, 32, 16, 8))

    def body(r_ref, o_ref):
        @pl.when(pl.program_id(0) == 0)
        def _():
            o_ref[...] = jnp.zeros_like(o_ref)

        o_ref[...] += jnp.broadcast_to(jnp.sum(r_ref[...], axis=0, keepdims=True), (8, 128))

    return pl.pallas_call(
        body, grid=(n // tm,), in_specs=[pl.BlockSpec((tm, 128), lambda i: (i, 0))],
        out_specs=pl.BlockSpec((8, 128), lambda i: (0, 0)), out_shape=jax.ShapeDtypeStruct((8, 128), F32),
        compiler_params=_cparams(("arbitrary",)), name="loss_total")(rows)


@jax.custom_vjp
def _total(rows):
    return _total_forward(rows)[0, 0]


def _total_fwd(rows):
    return _total_forward(rows)[0, 0], rows


def _total_bwd(rows, ct):
    lane = lax.broadcasted_iota(jnp.int32, rows.shape, 1)
    return (jnp.where(lane == 0, ct, 0.0).astype(F32),)


_total.defvjp(_total_fwd, _total_bwd)


def _pad_cols(a, width):
    return jnp.pad(a, ((0, 0), (0, width - a.shape[1])))


def _rw_cols(a):
    parts = [a[:, :3 * RW_W]]
    for i in range(4):
        parts.append(_pad_cols(a[:, 3 * RW_W + RANK * i:3 * RW_W + RANK * (i + 1)], 128))
    parts.append(a[:, 3 * RW_W + 4 * RANK:])
    return jnp.concatenate(parts, axis=1)


RW_COLS = 3 * RW_W + 4 * RANK + GATE_RANK
GD_COLS = 4 * GD_W + 4 * GD_H


def _split_w_in(w):
    gd = w[:, RW_COLS:RW_COLS + GD_COLS]
    rw = jnp.concatenate([_rw_cols(w[:, :RW_COLS]), _pad_cols(gd[:, 4 * GD_W:], 256)], axis=1)
    return w[:, RW_COLS + GD_COLS:], rw, gd[:, :4 * GD_W]


def _unshard(g, col):
    if col:
        return jnp.swapaxes(g, 0, 1).reshape(g.shape[1], NDEV * g.shape[2])
    return g.reshape(NDEV * g.shape[1], g.shape[2])


def _reshard(a, col):
    if col:
        return jnp.swapaxes(a.reshape(a.shape[0], NDEV, a.shape[1] // NDEV), 0, 1)
    return a.reshape(NDEV, a.shape[0] // NDEV, a.shape[1])


def _pack(parts, width, row_mult):
    flat = [p.reshape(-1) for p in parts]
    offs, o = [], 0
    for f in flat:
        offs.append(o)
        o += f.shape[0]
    rows = -(-o // (width * row_mult)) * row_mult
    cat = jnp.concatenate(flat + [jnp.zeros((rows * width - o,), flat[0].dtype)])
    return cat.reshape(rows, width), offs


BIG = ("w_in", "w_a_out", "w_b_out", "w_o", "ffn_w1", "ffn_w2")
LATE = BIG[1:]
SMALL = ("rw_w0", "rw_w_up", "rw_a0", "rw_a_up", "rw_g_up", "gd_conv_w", "ffn_conv_w")
ROW_SHARDED = ("w_o", "ffn_w2")
REPL = ("norm1_g", "norm2_g", "rw_mu", "rw_k_k", "rw_k_a", "rw_r_k", "rw_gn_g", "rw_gn_b", "gd_a_log", "gd_dt_bias",
        "gd_norm_g", "final_norm_g")
WEIGHTS = ('c_ctx', 'w_ada', 'b_ada', 'norm1_g', 'norm2_g', 'w_in', 'rw_mu', 'rw_k_k', 'rw_k_a', 'rw_r_k', 'rw_w0', 'rw_w_up', 'rw_a0', 'rw_a_up', 'rw_g_up', 'rw_gn_g', 'rw_gn_b', 'gd_conv_w', 'gd_a_log', 'gd_dt_bias', 'gd_norm_g', 'w_a_out', 'w_b_out', 'w_o', 'ffn_w1', 'ffn_conv_w', 'ffn_w2', 'final_norm_g')


def _view2d(a):
    return a.reshape(-1, a.shape[-1])


def _layer_loss(x, modl, modc, sinks, small, repl, ctx, target, big, n_ctx):
    sh1, sc1, g1, sh2, sc2, g2 = [modl[:, i * D:(i + 1) * D] for i in range(6)]
    csh1, csc1 = modc[:, :D], modc[:, D:2 * D]
    n1g, n2g, fg = repl["norm1_g"], repl["norm2_g"], repl["final_norm_g"].reshape(1, D)
    (h_lat,) = _rowwise(_f_norm_mod, "norm1_lat", tm=256)((x,), (n1g, sc1, sh1))
    (h_ctx,) = _rowwise(_f_norm_mod, "norm1_ctx", tm=256)((ctx,), (n1g, csc1, csh1))
    h = jnp.concatenate([h_ctx, h_lat], axis=0)
    wg, wr, wd = _split_w_in(big["w_in"])
    sg, sr, sd = _split_w_in(_unshard(sinks["w_in"], True))
    p_gate = _dense("proj_gate")(h_lat, wg, sg)
    p_rw = _dense("proj_rw")(h, wr, sr)
    p_gd = _dense("proj_gd")(h, wd, sd)
    n = h.shape[0]
    lat = slice(n_ctx, n)

    mu = _rw_cols(repl["rw_mu"])
    coef = jnp.concatenate([0.5 * mu, 1.0 - mu, 0.5 * mu], axis=0)
    pm = _stencil("rw_shift", ("seg", (n_ctx,)), ncols=RW_COLS + 128, tm=256, tc=768)(p_rw, coef)
    ab = p_rw[:, RW_COLS + 128:RW_COLS + 256]
    pad_rank = lambda a: jnp.pad(a, ((0, 0), (0, 128 - RANK), (0, 0))).reshape(256, RW_W)
    kk, w0, w1, kd0, kd1, ka0, ka1, g = _rowwise(_f_rw_prep, "rw_prep", tm=128)(
        (pm,), (repl["rw_k_k"], repl["rw_k_a"], small["rw_w0"], pad_rank(small["rw_w_up"]), small["rw_a0"],
                pad_rank(small["rw_a_up"]), small["rw_g_up"]))
    r, v = pm[:, :RW_W], pm[:, 2 * RW_W:3 * RW_W]
    o = _wkv_op(n_ctx)(r, v, kk, jnp.stack([w0, w1]), jnp.stack([kd0, kd1]), jnp.stack([ka0, ka1]))
    (ya,) = _rowwise(_f_rw_read, "rw_read", tm=256)(
        (o[0, lat], o[1, lat], r[lat], kd0[lat], kd1[lat], v[lat], g[lat]),
        (repl["rw_r_k"], repl["rw_gn_g"], repl["rw_gn_b"]))

    cq = _stencil("gd_conv", ("seg", (n_ctx,)), ncols=3 * GD_W, tm=256, tc=768)(p_gd, small["gd_conv_w"])
    z = p_gd[lat, 3 * GD_W:]
    lanes16 = lambda a: _pad_cols(a.reshape(1, 2 * GD_H), 128)
    q, k, v2, gb = _rowwise(_f_gd_prep, "gd_prep", tm=256)(
        (cq, ab), (lanes16(repl["gd_a_log"]), lanes16(repl["gd_dt_bias"])))
    per_head = lambda a: a.T.reshape(2, GD_H, n, 1)
    og, gathered, late_sinks = _gdn_op(n_ctx)(
        q, k, v2, per_head(gb[:, :2 * GD_H]), per_head(gb[:, 2 * GD_H:4 * GD_H]),
        tuple(big[k_] for k_ in LATE), tuple(sinks[k_] for k_ in LATE))
    (yb,) = _rowwise(_f_gd_read, "gd_read", tm=256)((og[0, lat], og[1, lat], z), (repl["gd_norm_g"],))
    w = {k_: _unshard(g_, k_ not in ROW_SHARDED) for k_, g_ in zip(LATE, gathered)}
    s = {k_: _unshard(s_, k_ not in ROW_SHARDED) for k_, s_ in zip(LATE, late_sinks)}

    za = _dense("a_out")(ya, w["w_a_out"], s["w_a_out"])
    zb = _dense("b_out")(yb, w["w_b_out"], s["w_b_out"])
    (merged,) = _rowwise(_f_merge, "merge", tm=256)((za, zb, p_gate), ())
    att = _dense("w_o")(merged, w["w_o"], s["w_o"])
    x1, h2 = _rowwise(_f_res_norm, "res_norm2", tm=256)((x, att), (g1, n2g, sc2, sh2))
    ug = _dense("ffn_gate")(h2, w["ffn_w1"][:, :D_FF], s["ffn_w1"][:, :D_FF])
    uv = _dense("ffn_val")(h2, w["ffn_w1"][:, D_FF:], s["ffn_w1"][:, D_FF:])
    gc = _stencil("ffn_conv", ("grid", GRID_W), ncols=D_FF, tm=256, tc=512)(ug, small["ffn_conv_w"])
    (act,) = _rowwise(_f_glu, "glu", tm=64)((gc, uv), ())
    ff = _dense("ffn_w2")(act, w["ffn_w2"], s["ffn_w2"])
    (rows,) = _rowwise(_f_final, "final", tm=256, n_nondiff=1)((x1, ff, target), (g2, fg))
    return _total(rows)


def kernel(x, c, ctx, c_ctx, w_ada, b_ada, norm1_g, norm2_g, w_in, rw_mu, rw_k_k, rw_k_a, rw_r_k, rw_w0, rw_w_up, rw_a0, rw_a_up, rw_g_up, rw_gn_g, rw_gn_b, gd_conv_w, gd_a_log, gd_dt_bias, gd_norm_g, w_a_out, w_b_out, w_o, ffn_w1, ffn_conv_w, ffn_w2, final_norm_g, loss_target, m_c_ctx, m_w_ada, m_b_ada, m_norm1_g, m_norm2_g, m_w_in, m_rw_mu, m_rw_k_k, m_rw_k_a, m_rw_r_k, m_rw_w0, m_rw_w_up, m_rw_a0, m_rw_a_up, m_rw_g_up, m_rw_gn_g, m_rw_gn_b, m_gd_conv_w, m_gd_a_log, m_gd_dt_bias, m_gd_norm_g, m_w_a_out, m_w_b_out, m_w_o, m_ffn_w1, m_ffn_conv_w, m_ffn_w2, m_final_norm_g, v_c_ctx, v_w_ada, v_b_ada, v_norm1_g, v_norm2_g, v_w_in, v_rw_mu, v_rw_k_k, v_rw_k_a, v_rw_r_k, v_rw_w0, v_rw_w_up, v_rw_a0, v_rw_a_up, v_rw_g_up, v_rw_gn_g, v_rw_gn_b, v_gd_conv_w, v_gd_a_log, v_gd_dt_bias, v_gd_norm_g, v_w_a_out, v_w_b_out, v_w_o, v_ffn_w1, v_ffn_conv_w, v_ffn_w2, v_final_norm_g):
    args = dict(x=x, c=c, ctx=ctx, c_ctx=c_ctx, w_ada=w_ada, b_ada=b_ada, norm1_g=norm1_g, norm2_g=norm2_g, w_in=w_in, rw_mu=rw_mu, rw_k_k=rw_k_k, rw_k_a=rw_k_a, rw_r_k=rw_r_k, rw_w0=rw_w0, rw_w_up=rw_w_up, rw_a0=rw_a0, rw_a_up=rw_a_up, rw_g_up=rw_g_up, rw_gn_g=rw_gn_g, rw_gn_b=rw_gn_b, gd_conv_w=gd_conv_w, gd_a_log=gd_a_log, gd_dt_bias=gd_dt_bias, gd_norm_g=gd_norm_g, w_a_out=w_a_out, w_b_out=w_b_out, w_o=w_o, ffn_w1=ffn_w1, ffn_conv_w=ffn_conv_w, ffn_w2=ffn_w2, final_norm_g=final_norm_g, loss_target=loss_target, m_c_ctx=m_c_ctx, m_w_ada=m_w_ada, m_b_ada=m_b_ada, m_norm1_g=m_norm1_g, m_norm2_g=m_norm2_g, m_w_in=m_w_in, m_rw_mu=m_rw_mu, m_rw_k_k=m_rw_k_k, m_rw_k_a=m_rw_k_a, m_rw_r_k=m_rw_r_k, m_rw_w0=m_rw_w0, m_rw_w_up=m_rw_w_up, m_rw_a0=m_rw_a0, m_rw_a_up=m_rw_a_up, m_rw_g_up=m_rw_g_up, m_rw_gn_g=m_rw_gn_g, m_rw_gn_b=m_rw_gn_b, m_gd_conv_w=m_gd_conv_w, m_gd_a_log=m_gd_a_log, m_gd_dt_bias=m_gd_dt_bias, m_gd_norm_g=m_gd_norm_g, m_w_a_out=m_w_a_out, m_w_b_out=m_w_b_out, m_w_o=m_w_o, m_ffn_w1=m_ffn_w1, m_ffn_conv_w=m_ffn_conv_w, m_ffn_w2=m_ffn_w2, m_final_norm_g=m_final_norm_g, v_c_ctx=v_c_ctx, v_w_ada=v_w_ada, v_b_ada=v_b_ada, v_norm1_g=v_norm1_g, v_norm2_g=v_norm2_g, v_w_in=v_w_in, v_rw_mu=v_rw_mu, v_rw_k_k=v_rw_k_k, v_rw_k_a=v_rw_k_a, v_rw_r_k=v_rw_r_k, v_rw_w0=v_rw_w0, v_rw_w_up=v_rw_w_up, v_rw_a0=v_rw_a0, v_rw_a_up=v_rw_a_up, v_rw_g_up=v_rw_g_up, v_rw_gn_g=v_rw_gn_g, v_rw_gn_b=v_rw_gn_b, v_gd_conv_w=v_gd_conv_w, v_gd_a_log=v_gd_a_log, v_gd_dt_bias=v_gd_dt_bias, v_gd_norm_g=v_gd_norm_g, v_w_a_out=v_w_a_out, v_w_b_out=v_w_b_out, v_w_o=v_w_o, v_ffn_w1=v_ffn_w1, v_ffn_conv_w=v_ffn_conv_w, v_ffn_w2=v_ffn_w2, v_final_norm_g=v_final_norm_g)
    me = 4 * lax.axis_index("x") + 2 * lax.axis_index("y") + lax.axis_index("c")
    x2, ctx2, target = args["x"][0], args["ctx"][0], args["loss_target"][0]
    shard = {k: _view2d(args[k][0]) for k in BIG + SMALL}

    small_pack, small_offs = _pack([args["c"]] + [shard[k] for k in SMALL], 128, 8)
    gathered = _exchange([shard["w_in"].astype(BF16), small_pack], [True, True], "gather_weights")
    small_all = gathered[-1].reshape(NDEV, -1)

    def unpack(allg, off, k):
        r_, c_ = shard[k].shape
        return _unshard(allg[:, off:off + r_ * c_].reshape(NDEV, r_, c_), k not in ROW_SHARDED)

    big = {k: shard[k].astype(BF16) for k in LATE}
    big["w_in"] = _unshard(gathered[0], True)
    small = {k: unpack(small_all, o, k) for k, o in zip(SMALL, small_offs[1:])}
    small["rw_w_up"] = small["rw_w_up"].reshape(2, RANK, RW_W)
    small["rw_a_up"] = small["rw_a_up"].reshape(2, RANK, RW_W)
    c_all = small_all[:, :D]

    c_ext = jnp.concatenate([c_all, args["c_ctx"].reshape(1, D), jnp.zeros((7, D), F32)], axis=0)
    w_ada = args["w_ada"][0]
    nb = w_ada.shape[1]
    mod_blk = _ada_forward(c_ext, w_ada, lax.dynamic_slice(args["b_ada"], (0, me * nb), (1, nb)))
    (mod_all,) = _exchange([mod_blk], [True], "gather_mod")
    mod_all = jnp.swapaxes(mod_all, 0, 1).reshape(16, NDEV * nb)
    modl = lax.dynamic_slice(mod_all, (me, 0), (1, NDEV * nb))
    modc = mod_all[NDEV:NDEV + 1]

    repl = {k: args[k] for k in REPL}
    sinks = {k: jnp.zeros((NDEV,) + shard[k].shape, GRAD_WIRE) for k in BIG}
    n_ctx = ctx2.shape[0]
    loss, grads = jax.value_and_grad(_layer_loss, argnums=(0, 1, 2, 3, 4, 5))(
        x2, modl, modc, sinks, small, repl, ctx2, target, big, n_ctx)
    dx, dmodl, dmodc, dbig, dsmall, drepl = grads
    loss = lax.psum(loss, ("x", "y", "c"))

    rep_pack, rep_offs = _pack([dmodl, dmodc] + [drepl[k] for k in REPL], 128, 8)
    (rep_all,) = _exchange([rep_pack], [True], "gather_small_grads")
    rep_all = rep_all.reshape(NDEV, -1)
    six_d = NDEV * nb
    dmod_ext = _ada_dmod(rep_all[:, :six_d], rep_all[:, six_d:2 * six_d])
    dmod_blk = lax.dynamic_slice(dmod_ext, (0, me * nb), (16, nb))
    (s_ext,) = _rowwise(_f_silu, "ada_silu", tm=16)((c_ext,), ())
    g_w_ada = _mm(s_ext, dmod_blk, ta=True, name="ada_dw")
    dpart = _mm(dmod_blk, w_ada, tb=True, name="ada_dc")
    (dpart,) = _rowwise(_f_silu_grad, "ada_dsilu", tm=16)((dpart, c_ext), ())

    flat_small = [_reshard(_view2d(dsmall[k]) if k != "ffn_conv_w" else dsmall[k], True).reshape(NDEV, -1) for k in SMALL]
    small_send = jnp.concatenate(flat_small, axis=1)
    pad = -small_send.shape[1] % 1024
    small_send = jnp.pad(small_send, ((0, 0), (0, pad))).reshape(NDEV, -1, 128)
    got = _exchange([dbig["w_in"], small_send, dpart[NDEV:NDEV + 1]], [False, False, True], "scatter_grads")
    small_got = got[1].reshape(NDEV, -1)

    slabs = {k: dbig[k] for k in LATE}
    slabs["w_in"] = got[0]
    o = 0
    for k in SMALL:
        r_, c_ = shard[k].shape
        slabs[k] = small_got[:, o:o + r_ * c_].reshape(NDEV, r_, c_)
        o += r_ * c_
    slabs["w_ada"] = g_w_ada[None]
    slabs["c_ctx"] = got[2]
    slabs["b_ada"] = jnp.concatenate([rep_all[:, :six_d], rep_all[:, six_d:2 * six_d]], axis=0)[:, None, :]
    for k, off in zip(REPL, rep_offs[2:]):
        size = args[k].size
        slabs[k] = rep_all[:, off:off + size].reshape((NDEV,) + _view2d(args[k]).shape)

    outs = {}
    for k in WEIGHTS:
        shape = args[k].shape
        res = _adamw(slabs[k], _view2d(args[k]), _view2d(args["m_" + k]), _view2d(args["v_" + k]), "adamw_" + k)
        outs[k] = [a.reshape(shape) for a in res]
    return (loss, dx[None]) + tuple(outs[k][i] for i in range(4) for k in WEIGHTS)
```

```python
import functools
import math

import jax
import jax.numpy as jnp
from jax import lax
from jax.experimental import pallas as pl
from jax.experimental.pallas import tpu as pltpu

F32 = jnp.float32
BF16 = jnp.bfloat16
HI = lax.Precision.HIGHEST
GRAD_WIRE = BF16

D = 2048
NCTX = 256
GRID_W = 64
NORM_EPS = 1e-6
RW_H, RW_N = 16, 64
RW_W = RW_H * RW_N
RANK = 96
GATE_RANK = 256
RW_GN_EPS = 64e-5
GD_H, GD_N = 8, 128
GD_W = GD_H * GD_N
CHUNK = 64
D_FF = 5632
NDEV = 8
PW = 4096
VMEM_LIMIT = 56 * 1024 * 1024
MM_VMEM_BUDGET = 46 * 1024 * 1024

ADAM_LR, ADAM_B1, ADAM_B2, ADAM_EPS, ADAM_WD, ADAM_STEP = 0.001, 0.9, 0.999, 1e-08, 0.01, 10


def _cparams(sem):
    return pltpu.CompilerParams(dimension_semantics=sem, vmem_limit_bytes=VMEM_LIMIT)


def _pick(n, cands):
    for c in cands:
        if n % c == 0:
            return c
    return n


def _mm(a, b, *, ta=False, tb=False, name, out_dtype=F32):
    M = a.shape[1] if ta else a.shape[0]
    K = a.shape[0] if ta else a.shape[1]
    N = b.shape[0] if tb else b.shape[1]
    assert K == (b.shape[1] if tb else b.shape[0])
    tm = _pick(M, (512, 768, 256, 128, 64, 16, 8))

    def vmem_bytes(tn_):
        return (2 * tm * K * a.dtype.itemsize + tm * K * 2 + 2 * tn_ * K * b.dtype.itemsize
                + 2 * tm * tn_ * jnp.dtype(out_dtype).itemsize)

    tn = next(t for t in (1024, 512, 256, 128) if N % t == 0 and (vmem_bytes(t) <= MM_VMEM_BUDGET or t == 128))
    dn = (((0 if ta else 1,), (1 if tb else 0,)), ((), ()))

    def body(a_ref, b_ref, o_ref, a16):
        @pl.when(pl.program_id(1) == 0)
        def _():
            a16[...] = a_ref[...].astype(BF16)

        o_ref[...] = lax.dot_general(a16[...], b_ref[...].astype(BF16), dn,
                                     preferred_element_type=F32).astype(out_dtype)

    a_blk = (K, tm) if ta else (tm, K)
    a_spec = pl.BlockSpec(a_blk, (lambda i, j: (0, i)) if ta else (lambda i, j: (i, 0)))
    b_spec = pl.BlockSpec((tn, K), lambda i, j: (j, 0)) if tb else pl.BlockSpec((K, tn), lambda i, j: (0, j))
    return pl.pallas_call(
        body, grid=(M // tm, N // tn), in_specs=[a_spec, b_spec],
        out_specs=pl.BlockSpec((tm, tn), lambda i, j: (i, j)),
        out_shape=jax.ShapeDtypeStruct((M, N), out_dtype),
        scratch_shapes=[pltpu.VMEM(a_blk, BF16)],
        compiler_params=_cparams(("parallel", "arbitrary")), name=name)(a, b)


def _dense(name):
    @jax.custom_vjp
    def f(a, w, sink):
        return _mm(a, w, name=name + "_fwd")

    def fwd(a, w, sink):
        return _mm(a, w, name=name + "_fwd"), (a, w)

    def bwd(res, dc):
        a, w = res
        return (_mm(dc, w, tb=True, name=name + "_da"), None,
                _mm(a, dc, ta=True, name=name + "_dw", out_dtype=GRAD_WIRE))

    f.defvjp(fwd, bwd)
    return f


def _rowwise(fn, name, *, tm, n_nondiff=0):
    def shapes(rows, params):
        tiles = [jax.ShapeDtypeStruct((tm, r.shape[1]), r.dtype) for r in rows]
        ps = [jax.ShapeDtypeStruct(p.shape, p.dtype) for p in params]
        return tiles, ps, jax.eval_shape(fn, *tiles, *ps)

    def row_spec(width):
        return pl.BlockSpec((tm, width), lambda i: (i, 0))

    def whole_spec(shape):
        return pl.BlockSpec(shape, lambda i: (0,) * len(shape))

    def forward(rows, params):
        n = rows[0].shape[0]
        tiles, ps, outs = shapes(rows, params)
        nin = len(rows) + len(params)

        def body(*refs):
            res = fn(*[r[...] for r in refs[:nin]])
            for o_ref, v in zip(refs[nin:], res):
                o_ref[...] = v.astype(o_ref.dtype)

        return pl.pallas_call(
            body, grid=(n // tm,),
            in_specs=[row_spec(t.shape[1]) for t in tiles] + [whole_spec(p.shape) for p in ps],
            out_specs=[row_spec(o.shape[1]) for o in outs],
            out_shape=[jax.ShapeDtypeStruct((n, o.shape[1]), o.dtype) for o in outs],
            compiler_params=_cparams(("parallel",)), name=name + "_fwd")(*rows, *params)

    def backward(rows, params, cts):
        n = rows[0].shape[0]
        tiles, ps, outs = shapes(rows, params)
        nr, npar, nout = len(rows), len(params), len(outs)
        nd = nr - n_nondiff

        def body(*refs):
            ins = [r[...] for r in refs[:nr + npar]]
            ct = tuple(r[...] for r in refs[nr + npar:nr + npar + nout])
            out_refs = refs[nr + npar + nout:]
            fixed = ins[nd:nr]

            def g(*diff):
                return fn(*diff[:nd], *fixed, *diff[nd:])

            _, vjp = jax.vjp(g, *ins[:nd], *ins[nr:])
            grads = vjp(ct)
            for o_ref, v in zip(out_refs[:nd], grads[:nd]):
                o_ref[...] = v.astype(o_ref.dtype)

            @pl.when(pl.program_id(0) == 0)
            def _():
                for o_ref in out_refs[nd:]:
                    o_ref[...] = jnp.zeros_like(o_ref)

            for o_ref, v in zip(out_refs[nd:], grads[nd:]):
                o_ref[...] += v

        res = pl.pallas_call(
            body, grid=(n // tm,),
            in_specs=[row_spec(t.shape[1]) for t in tiles] + [whole_spec(p.shape) for p in ps]
            + [row_spec(o.shape[1]) for o in outs],
            out_specs=[row_spec(t.shape[1]) for t in tiles[:nd]] + [whole_spec(p.shape) for p in ps],
            out_shape=[jax.ShapeDtypeStruct((n, t.shape[1]), t.dtype) for t in tiles[:nd]]
            + [jax.ShapeDtypeStruct(p.shape, p.dtype) for p in ps],
            compiler_params=_cparams(("arbitrary",)), name=name + "_bwd")(*rows, *params, *cts)
        return tuple(res[:nd]) + (None,) * n_nondiff, tuple(res[nd:])

    @jax.custom_vjp
    def op(rows, params):
        return tuple(forward(rows, params))

    def op_fwd(rows, params):
        return tuple(forward(rows, params)), (rows, params)

    def op_bwd(res, cts):
        return backward(res[0], res[1], cts)

    op.defvjp(op_fwd, op_bwd)
    return op


def _valid(t, off, n, mode):
    s = t + off[0]
    ok = (s >= 0) & (s < n)
    if mode[0] == "seg":
        for b in mode[1]:
            ok = ok & ((t >= b) == (s >= b))
    else:
        col = lax.rem(t, mode[1]) + off[1]
        ok = ok & (col >= 0) & (col < mode[1])
    return ok


def _stencil_offsets(mode):
    if mode[0] == "seg":
        return [(-1, 0), (0, 0), (1, 0)]
    w = mode[1]
    return [(di * w + dj, dj) for di in (-1, 0, 1) for dj in (-1, 0, 1)]


def _stencil(name, mode, *, ncols, tm, tc):
    offs = _stencil_offsets(mode)
    J = len(offs)
    halo = 8 if mode[0] == "seg" else 128

    def x_specs(n):
        nb = n // halo
        r = tm // halo
        return [pl.BlockSpec((halo, tc), lambda i, j: (jnp.maximum(i * r - 1, 0), j)),
                pl.BlockSpec((tm, tc), lambda i, j: (i, j)),
                pl.BlockSpec((halo, tc), lambda i, j: (jnp.minimum((i + 1) * r, nb - 1), j))]

    def fill(buf, prev, cur, nxt):
        buf[0:halo, :] = prev[...]
        buf[halo:halo + tm, :] = cur[...]
        buf[halo + tm:, :] = nxt[...]

    def forward(x, coef, suffix=""):
        n = x.shape[0]

        def body(prev, cur, nxt, c_ref, o_ref, buf):
            fill(buf, prev, cur, nxt)
            t = pl.program_id(0) * tm + lax.broadcasted_iota(jnp.int32, (tm, 1), 0)
            acc = jnp.zeros((tm, tc), F32)
            for j, off in enumerate(offs):
                xs = buf[halo + off[0]:halo + off[0] + tm, :]
                acc = acc + jnp.where(_valid(t, off, n, mode), xs, 0.0) * c_ref[j:j + 1, :]
            o_ref[...] = acc

        return pl.pallas_call(
            body, grid=(n // tm, ncols // tc),
            in_specs=x_specs(n) + [pl.BlockSpec((J, tc), lambda i, j: (0, j))],
            out_specs=pl.BlockSpec((tm, tc), lambda i, j: (i, j)),
            out_shape=jax.ShapeDtypeStruct((n, ncols), F32),
            scratch_shapes=[pltpu.VMEM((tm + 2 * halo, tc), F32)],
            compiler_params=_cparams(("parallel", "parallel")), name=name + suffix)(x, x, x, coef)

    def wgrad(x, dy):
        n = x.shape[0]

        def body(prev, cur, nxt, dy_ref, o_ref, buf):
            fill(buf, prev, cur, nxt)
            i = pl.program_id(1)
            t = i * tm + lax.broadcasted_iota(jnp.int32, (tm, 1), 0)

            @pl.when(i == 0)
            def _():
                o_ref[...] = jnp.zeros_like(o_ref)

            dy = dy_ref[...]
            for j, off in enumerate(offs):
                xs = buf[halo + off[0]:halo + off[0] + tm, :]
                o_ref[j:j + 1, :] += jnp.sum(jnp.where(_valid(t, off, n, mode), xs, 0.0) * dy, axis=0, keepdims=True)

        specs = [pl.BlockSpec(s.block_shape, (lambda f: lambda j, i: f(i, j))(s.index_map)) for s in x_specs(n)]
        return pl.pallas_call(
            body, grid=(ncols // tc, n // tm),
            in_specs=specs + [pl.BlockSpec((tm, tc), lambda j, i: (i, j))],
            out_specs=pl.BlockSpec((J, tc), lambda j, i: (0, j)),
            out_shape=jax.ShapeDtypeStruct((J, ncols), F32),
            scratch_shapes=[pltpu.VMEM((tm + 2 * halo, tc), F32)],
            compiler_params=_cparams(("parallel", "arbitrary")), name=name + "_wgrad")(x, x, x, dy)

    @jax.custom_vjp
    def op(x, coef):
        return forward(x, coef)

    def op_fwd(x, coef):
        return forward(x, coef), (x, coef)

    def op_bwd(res, dy):
        x, coef = res
        dx = forward(dy, coef[::-1], "_adj")
        if x.shape[1] != ncols:
            dx = jnp.pad(dx, ((0, 0), (0, x.shape[1] - ncols)))
        return dx, wgrad(x, dy)

    op.defvjp(op_fwd, op_bwd)
    return op


WKV_TC = 32
WKV_TB = 64


def _chunk_index(d, c, n_ctx_chunks, n_chunks):
    rev = jnp.where(c < n_ctx_chunks, n_ctx_chunks - 1 - c, n_ctx_chunks + n_chunks - 1 - c)
    return jnp.where(d == 0, c, rev)


def _to_feature_major(a):
    n = a.shape[0]
    t = a.reshape(n // WKV_TB, WKV_TB, RW_H // 2, 2, RW_N).transpose(2, 4, 0, 3, 1)
    return t.reshape(RW_W // 2, n // WKV_TB * 128)


def _from_feature_major(t):
    n = t.shape[1] // 128 * WKV_TB
    a = t.reshape(RW_H // 2, RW_N, n // WKV_TB, 2, WKV_TB).transpose(2, 4, 0, 3, 1)
    return a.reshape(n, RW_W)


def _pair_consts():
    lane = lax.broadcasted_iota(jnp.int32, (1, 128), 1)
    first = lax.broadcasted_iota(jnp.int32, (RW_N, 128), 1) < RW_N
    same = (lax.div(lax.broadcasted_iota(jnp.int32, (128, 128), 0), RW_N)
            == lax.div(lax.broadcasted_iota(jnp.int32, (128, 128), 1), RW_N))
    return lane < RW_N, lane >= RW_N, first, same.astype(BF16)


def _seg_reduce(x, row, m0, m1, first):
    s0 = jnp.sum(x * jnp.where(m0, row, 0.0), axis=1, keepdims=True)
    s1 = jnp.sum(x * jnp.where(m1, row, 0.0), axis=1, keepdims=True)
    return jnp.where(first, s0, s1)


def _seg_sum_mxu(x, same, passes):
    acc, rest = None, x
    for i in range(passes):
        piece = rest.astype(BF16)
        part = jnp.dot(piece, same, preferred_element_type=F32)
        acc = part if acc is None else acc + part
        if i + 1 < passes:
            rest = rest - piece.astype(F32)
    return acc


def _wkv_forward(r, vT, kk, w, kd, ka, n_ctx):
    n = r.shape[0]
    nc, ncc = n // WKV_TC, n_ctx // WKV_TC
    per_blk = WKV_TB // WKV_TC

    def body(r_ref, vT_ref, kk_ref, w_ref, kd_ref, ka_ref, oT_ref, st_ref, S_ref):
        d, c = pl.program_id(0), pl.program_id(1)
        ci = _chunk_index(d, c, ncc, nc)

        @pl.when(c == 0)
        def _():
            S_ref[...] = jnp.zeros_like(S_ref)

        @pl.when(lax.rem(c, per_blk) == 0)
        def _():
            oT_ref[...] = jnp.zeros_like(oT_ref)

        m0, m1, first, same = _pair_consts()
        lane_t = lax.rem(lax.broadcasted_iota(jnp.int32, (RW_N, 128), 1), WKV_TB)

        def step(s, carry):
            tl = jnp.where(d == 0, s, WKV_TC - 1 - s)
            sel = lane_t == lax.rem(ci, per_blk) * WKV_TC + tl
            row = pl.ds(tl, 1)
            kn_all, w_all, kd_all, ka_all, r_all = -kk_ref[row, :], w_ref[row, :], kd_ref[row, :], ka_ref[row, :], r_ref[row, :]
            pairs = range(RW_H // 2)
            cs = [slice(p * 128, (p + 1) * 128) for p in pairs]
            rows = [slice(p * RW_N, (p + 1) * RW_N) for p in pairs]
            S = [S_ref[p] for p in pairs]
            for p in pairs:
                st_ref[tl, p] = S[p]
            sab = [_seg_reduce(S[p], kn_all[:, cs[p]], m0, m1, first) for p in pairs]
            vb = [_seg_sum_mxu(jnp.where(sel, vT_ref[rows[p], :], 0.0), same, 2) for p in pairs]
            S = [S[p] * w_all[:, cs[p]] + sab[p] * ka_all[:, cs[p]] + vb[p] * kd_all[:, cs[p]] for p in pairs]
            for p in pairs:
                S_ref[p] = S[p]
            for p in pairs:
                ob = _seg_reduce(S[p], r_all[:, cs[p]], m0, m1, first)
                oT_ref[rows[p], :] = jnp.where(sel, ob, oT_ref[rows[p], :])
            return carry

        lax.fori_loop(0, WKV_TC, step, 0)

    def row(dirn):
        if dirn:
            return pl.BlockSpec((None, WKV_TC, RW_W), lambda d, c: (d, _chunk_index(d, c, ncc, nc), 0))
        return pl.BlockSpec((WKV_TC, RW_W), lambda d, c: (_chunk_index(d, c, ncc, nc), 0))

    colT = pl.BlockSpec((RW_W // 2, 128), lambda d, c: (0, lax.div(_chunk_index(d, c, ncc, nc), per_blk)))
    return pl.pallas_call(
        body, grid=(2, nc),
        in_specs=[row(0), colT, row(0), row(1), row(1), row(1)],
        out_specs=[pl.BlockSpec((None, RW_W // 2, 128), lambda d, c: (d, 0, lax.div(_chunk_index(d, c, ncc, nc), per_blk))),
                   pl.BlockSpec((None, WKV_TC, RW_H // 2, RW_N, 128),
                                lambda d, c: (d, _chunk_index(d, c, ncc, nc), 0, 0, 0))],
        out_shape=[jax.ShapeDtypeStruct((2, RW_W // 2, n // WKV_TB * 128), F32),
                   jax.ShapeDtypeStruct((2, n, RW_H // 2, RW_N, 128), F32)],
        scratch_shapes=[pltpu.VMEM((RW_H // 2, RW_N, 128), F32)],
        compiler_params=_cparams(("arbitrary", "arbitrary")), name="wkv_fwd")(r, vT, kk, w, kd, ka)


def _wkv_backward(r, vT, kk, w, kd, ka, st, doT, n_ctx):
    n = r.shape[0]
    nc, ncc = n // WKV_TC, n_ctx // WKV_TC
    per_blk = WKV_TB // WKV_TC

    def body(r_ref, vT_ref, kk_ref, w_ref, kd_ref, ka_ref, st_ref, doT_ref,
             dr_ref, dw_ref, dkd_ref, dkn_ref, dka_ref, dvT_ref, dS_ref):
        d, c = pl.program_id(0), pl.program_id(1)
        ci = _chunk_index(d, nc - 1 - c, ncc, nc)

        @pl.when(c == 0)
        def _():
            dS_ref[...] = jnp.zeros_like(dS_ref)

        @pl.when(lax.rem(c, per_blk) == 0)
        def _():
            dvT_ref[...] = jnp.zeros_like(dvT_ref)

        m0, m1, first, same = _pair_consts()
        lane_t = lax.rem(lax.broadcasted_iota(jnp.int32, (RW_N, 128), 1), WKV_TB)

        def step(s, carry):
            tl = jnp.where(d == 0, WKV_TC - 1 - s, s)
            sel = lane_t == lax.rem(ci, per_blk) * WKV_TC + tl
            row = pl.ds(tl, 1)
            kn_all, w_all, kd_all, ka_all, r_all = -kk_ref[row, :], w_ref[row, :], kd_ref[row, :], ka_ref[row, :], r_ref[row, :]
            pairs = range(RW_H // 2)
            cs = [slice(p * 128, (p + 1) * 128) for p in pairs]
            rows = [slice(p * RW_N, (p + 1) * RW_N) for p in pairs]
            colsum = lambda a: jnp.sum(a, axis=0, keepdims=True)
            Sp = [st_ref[tl, p] for p in pairs]
            dob = [_seg_sum_mxu(jnp.where(sel, doT_ref[rows[p], :], 0.0), same, 2) for p in pairs]
            dS = [dS_ref[p] + dob[p] * r_all[:, cs[p]] for p in pairs]
            dsab = [_seg_reduce(dS[p], ka_all[:, cs[p]], m0, m1, first) for p in pairs]
            sab = [_seg_reduce(Sp[p], kn_all[:, cs[p]], m0, m1, first) for p in pairs]
            vb = [_seg_sum_mxu(jnp.where(sel, vT_ref[rows[p], :], 0.0), same, 2) for p in pairs]
            St = [Sp[p] * w_all[:, cs[p]] + sab[p] * ka_all[:, cs[p]] + vb[p] * kd_all[:, cs[p]] for p in pairs]
            dr_ref[row, :] = jnp.concatenate([colsum(St[p] * dob[p]) for p in pairs], axis=1)
            dw_ref[row, :] = jnp.concatenate([colsum(dS[p] * Sp[p]) for p in pairs], axis=1)
            dka_ref[row, :] = jnp.concatenate([colsum(dS[p] * sab[p]) for p in pairs], axis=1)
            dkd_ref[row, :] = jnp.concatenate([colsum(dS[p] * vb[p]) for p in pairs], axis=1)
            dkn_ref[row, :] = jnp.concatenate([colsum(Sp[p] * dsab[p]) for p in pairs], axis=1)
            for p in pairs:
                dvb = _seg_reduce(dS[p], kd_all[:, cs[p]], m0, m1, first)
                dvT_ref[rows[p], :] = jnp.where(sel, dvb, dvT_ref[rows[p], :])
                dS_ref[p] = dS[p] * w_all[:, cs[p]] + dsab[p] * kn_all[:, cs[p]]
            return carry

        lax.fori_loop(0, WKV_TC, step, 0)

    def cidx(d, c):
        return _chunk_index(d, nc - 1 - c, ncc, nc)

    def row(dirn):
        if dirn:
            return pl.BlockSpec((None, WKV_TC, RW_W), lambda d, c: (d, cidx(d, c), 0))
        return pl.BlockSpec((WKV_TC, RW_W), lambda d, c: (cidx(d, c), 0))

    colT = pl.BlockSpec((RW_W // 2, 128), lambda d, c: (0, lax.div(cidx(d, c), per_blk)))
    colT_d = pl.BlockSpec((None, RW_W // 2, 128), lambda d, c: (d, 0, lax.div(cidx(d, c), per_blk)))
    st_spec = pl.BlockSpec((None, WKV_TC, RW_H // 2, RW_N, 128), lambda d, c: (d, cidx(d, c), 0, 0, 0))
    rows_out = jax.ShapeDtypeStruct((2, n, RW_W), F32)
    return pl.pallas_call(
        body, grid=(2, nc),
        in_specs=[row(0), colT, row(0), row(1), row(1), row(1), st_spec, colT_d],
        out_specs=[row(1)] * 5 + [colT_d],
        out_shape=[rows_out] * 5 + [jax.ShapeDtypeStruct((2, RW_W // 2, n // WKV_TB * 128), F32)],
        scratch_shapes=[pltpu.VMEM((RW_H // 2, RW_N, 128), F32)],
        compiler_params=_cparams(("arbitrary", "arbitrary")), name="wkv_bwd")(r, vT, kk, w, kd, ka, st, doT)


def _wkv_op(n_ctx):
    def readout(oT):
        return jnp.stack([_from_feature_major(oT[0]), _from_feature_major(oT[1])])

    @jax.custom_vjp
    def op(r, v, kk, w, kd, ka):
        return readout(_wkv_forward(r, _to_feature_major(v), kk, w, kd, ka, n_ctx)[0])

    def fwd(r, v, kk, w, kd, ka):
        vT = _to_feature_major(v)
        oT, st = _wkv_forward(r, vT, kk, w, kd, ka, n_ctx)
        return readout(oT), (r, vT, kk, w, kd, ka, st)

    def bwd(res, do):
        r, vT, kk, w, kd, ka, st = res
        doT = jnp.stack([_to_feature_major(do[0]), _to_feature_major(do[1])])
        dr, dw, dkd, dkn, dka, dvT = _wkv_backward(r, vT, kk, w, kd, ka, st, doT, n_ctx)
        return dr[0] + dr[1], _from_feature_major(dvT[0] + dvT[1]), -(dkn[0] + dkn[1]), dw, dkd, dka

    op.defvjp(fwd, bwd)
    return op


GD_HB = 4


def _dot(a, b, dims=((1,), (0,))):
    return lax.dot_general(a, b, (dims, ((), ())), precision=HI, preferred_element_type=F32)


def _gdn_chunk(S, q, k, v, g, beta, rev):
    R = q.shape[0]
    C = CHUNK
    nh = R // C
    ri = lax.broadcasted_iota(jnp.int32, (R, R), 0)
    cj = lax.broadcasted_iota(jnp.int32, (R, R), 1)
    same = lax.div(ri, C) == lax.div(cj, C)
    lag = (ri - cj) * (1 - 2 * rev.astype(jnp.int32))
    incl = same & (lag >= 0)
    strict = same & (lag > 0)
    eye = (ri == cj).astype(F32)
    gb = jnp.broadcast_to(g, (R, GD_N))
    bb = jnp.broadcast_to(beta, (R, GD_N))
    G = _dot(incl.astype(F32), gb)
    Gc = jnp.concatenate([G] * (R // GD_N), axis=1) if R > GD_N else G[:, :R]
    Grow = _dot(jnp.ones((R, R), F32), eye * Gc)
    decay = jnp.where(incl, jnp.exp(jnp.where(incl, Gc - Grow, 0.0)), 0.0)
    kb = k * bb
    A = jnp.where(strict, _dot(kb, k, ((1,), (1,))) * decay, 0.0)
    Nk = -A
    T = eye + Nk
    for _ in range(int(math.log2(C)) - 1):
        Nk = _dot(Nk, Nk)
        T = T + _dot(T, Nk)
    u = _dot(T, v * bb)
    w = _dot(T, kb * jnp.exp(G))
    attn = jnp.where(incl, _dot(q, k, ((1,), (1,))) * decay, 0.0)
    head = lax.div(lax.broadcasted_iota(jnp.int32, (R, GD_N), 0), C)

    def spread(a):
        return jnp.concatenate([jnp.where(head == h, a, 0.0) for h in range(nh)], axis=1)

    v_new = u - _dot(spread(w), S)
    o = _dot(spread(q * jnp.exp(G)), S) + _dot(attn, v_new)
    Gtot = _dot(same.astype(F32), gb)
    k_dec = k * jnp.exp(Gtot - G)
    Gs = jnp.concatenate([Gtot[h * C:(h + 1) * C] for h in range(nh) for _ in range(GD_N // C)], axis=0)
    S_new = S * jnp.exp(Gs) + _dot(spread(k_dec), v_new, ((0,), (0,)))
    return S_new, o


def _stack_heads(ref):
    return jnp.concatenate([ref[:, h * GD_N:(h + 1) * GD_N] for h in range(GD_HB)], axis=0)


def _unstack_heads(ref, a):
    for h in range(GD_HB):
        ref[:, h * GD_N:(h + 1) * GD_N] = a[h * CHUNK:(h + 1) * CHUNK]


def _gdn_specs(n, n_ctx, back):
    nc, ncc = n // CHUNK, n_ctx // CHUNK

    def ci(d, c):
        return _chunk_index(d, nc - 1 - c if back else c, ncc, nc)

    tok = pl.BlockSpec((CHUNK, GD_HB * GD_N), lambda d, h, c: (ci(d, c), h))
    tok_d = pl.BlockSpec((None, CHUNK, GD_HB * GD_N), lambda d, h, c: (d, ci(d, c), h))
    col = pl.BlockSpec((None, GD_HB, CHUNK, 1), lambda d, h, c: (d, h, ci(d, c), 0))
    st = pl.BlockSpec((None, GD_HB, None, GD_N, GD_N), lambda d, h, c: (d, h, ci(d, c), 0, 0))
    return nc, tok, tok_d, col, st


def _ride(ride, bcast, first, last, refs_in, refs_out, sems):
    if not ride:
        return
    start, wait = _exchange_copies(refs_in, refs_out, [bcast] * len(ride), *sems)
    pl.when(first)(start)
    pl.when(last)(wait)


def _grid_ends(nc):
    d, h, c = pl.program_id(0), pl.program_id(1), pl.program_id(2)
    first = (d == 0) & (h == 0) & (c == 0)
    last = (d == 1) & (h == GD_H // GD_HB - 1) & (c == nc - 1)
    return first, last


def _gdn_forward(q, k, v, g, beta, n_ctx, ride=()):
    n = q.shape[0]
    nc, tok, tok_d, col, st = _gdn_specs(n, n_ctx, False)
    nr = len(ride)

    def body(*refs):
        q_ref, k_ref, v_ref, g_ref, b_ref = refs[:5]
        o_ref, st_ref = refs[5 + nr:7 + nr]
        S_ref = refs[7 + 2 * nr]
        first, last = _grid_ends(nc)
        _ride(ride, True, first, last, refs[5:5 + nr], refs[7 + nr:7 + 2 * nr], refs[8 + 2 * nr:])

        @pl.when(pl.program_id(2) == 0)
        def _():
            S_ref[...] = jnp.zeros_like(S_ref)

        S = S_ref[...]
        st_ref[...] = S.reshape(GD_HB, GD_N, GD_N)
        S_new, o = _gdn_chunk(S, _stack_heads(q_ref), _stack_heads(k_ref), _stack_heads(v_ref),
                              g_ref[...].reshape(GD_HB * CHUNK, 1), b_ref[...].reshape(GD_HB * CHUNK, 1),
                              pl.program_id(0) == 1)
        S_ref[...] = S_new
        _unstack_heads(o_ref, o)

    any_spec = pl.BlockSpec(memory_space=pl.ANY)
    res = pl.pallas_call(
        body, grid=(2, GD_H // GD_HB, nc), in_specs=[tok, tok, tok, col, col] + [any_spec] * nr,
        out_specs=[tok_d, st] + [any_spec] * nr,
        out_shape=[jax.ShapeDtypeStruct((2, n, GD_W), F32), jax.ShapeDtypeStruct((2, GD_H, nc, GD_N, GD_N), F32)]
        + _exchange_out_shapes(ride, [True] * nr),
        scratch_shapes=[pltpu.VMEM((GD_HB * GD_N, GD_N), F32)] + (_exchange_sems(nr) if nr else []),
        compiler_params=_cparams(("arbitrary", "arbitrary", "arbitrary")), name="gdn_fwd")(q, k, v, g, beta, *ride)
    return res[0], res[1], tuple(res[2:])


def _gdn_backward(q, k, v, g, beta, st, do, n_ctx, ride=()):
    n = q.shape[0]
    nc, tok, tok_d, col, st_spec = _gdn_specs(n, n_ctx, True)
    nr = len(ride)

    def body(*refs):
        q_ref, k_ref, v_ref, g_ref, b_ref, st_ref, do_ref = refs[:7]
        dq_ref, dk_ref, dv_ref, dg_ref, db_ref = refs[7 + nr:12 + nr]
        dS_ref = refs[12 + 2 * nr]
        first, last = _grid_ends(nc)
        _ride(ride, False, first, last, refs[7:7 + nr], refs[12 + nr:12 + 2 * nr], refs[13 + 2 * nr:])

        @pl.when(pl.program_id(2) == 0)
        def _():
            dS_ref[...] = jnp.zeros_like(dS_ref)

        rev = pl.program_id(0) == 1
        rows = GD_HB * CHUNK
        _, vjp = jax.vjp(lambda S, q_, k_, v_, g_, b_: _gdn_chunk(S, q_, k_, v_, g_, b_, rev),
                         st_ref[...].reshape(GD_HB * GD_N, GD_N), _stack_heads(q_ref), _stack_heads(k_ref),
                         _stack_heads(v_ref), g_ref[...].reshape(rows, 1), b_ref[...].reshape(rows, 1))
        dS, dq, dk, dv, dg, db = vjp((dS_ref[...], _stack_heads(do_ref)))
        dS_ref[...] = dS
        _unstack_heads(dq_ref, dq)
        _unstack_heads(dk_ref, dk)
        _unstack_heads(dv_ref, dv)
        dg_ref[...] = dg.reshape(GD_HB, CHUNK, 1)
        db_ref[...] = db.reshape(GD_HB, CHUNK, 1)

    tok_out = jax.ShapeDtypeStruct((2, n, GD_W), F32)
    col_out = jax.ShapeDtypeStruct((2, GD_H, n, 1), F32)
    any_spec = pl.BlockSpec(memory_space=pl.ANY)
    res = pl.pallas_call(
        body, grid=(2, GD_H // GD_HB, nc), in_specs=[tok, tok, tok, col, col, st_spec, tok_d] + [any_spec] * nr,
        out_specs=[tok_d, tok_d, tok_d, col, col] + [any_spec] * nr,
        out_shape=[tok_out, tok_out, tok_out, col_out, col_out] + _exchange_out_shapes(ride, [False] * nr),
        scratch_shapes=[pltpu.VMEM((GD_HB * GD_N, GD_N), F32)] + (_exchange_sems(nr) if nr else []),
        compiler_params=_cparams(("arbitrary", "arbitrary", "arbitrary")), name="gdn_bwd")(
            q, k, v, g, beta, st, do, *ride)
    return tuple(res[:5]), tuple(res[5:])


def _gdn_op(n_ctx):
    @jax.custom_vjp
    def op(q, k, v, g, beta, shards, sinks):
        o, _, gathered = _gdn_forward(q, k, v, g, beta, n_ctx, shards)
        return o, gathered, sinks

    def fwd(q, k, v, g, beta, shards, sinks):
        o, st, gathered = _gdn_forward(q, k, v, g, beta, n_ctx, shards)
        return (o, gathered, sinks), (q, k, v, g, beta, st, shards)

    def bwd(res, cts):
        q, k, v, g, beta, st, shards = res
        do, _, dsinks = cts
        (dq, dk, dv, dg, db), received = _gdn_backward(q, k, v, g, beta, st, do, n_ctx, tuple(dsinks))
        return dq[0] + dq[1], dk[0] + dk[1], dv[0] + dv[1], dg, db, tuple(None for _ in shards), received

    op.defvjp(fwd, bwd)
    return op


def _bdot(a, b):
    return jnp.dot(a.astype(BF16), b.astype(BF16), preferred_element_type=F32)


def _rms(x, eps=NORM_EPS):
    return x * lax.rsqrt(jnp.mean(x * x, axis=-1, keepdims=True) + eps)


def _softplus(x):
    return jnp.maximum(x, 0.0) + jnp.log(1.0 + jnp.exp(-jnp.abs(x)))


def _heads(width, seg):
    e = (lax.div(lax.broadcasted_iota(jnp.int32, (width, 128), 0), seg)
         == lax.broadcasted_iota(jnp.int32, (width, 128), 1)).astype(F32)
    et = (lax.div(lax.broadcasted_iota(jnp.int32, (128, width), 1), seg)
          == lax.broadcasted_iota(jnp.int32, (128, width), 0)).astype(F32)
    return e, et


def _head_sum(x, seg):
    e, et = _heads(x.shape[1], seg)
    return _dot(_dot(x, e), et)


def _l2n(x, seg):
    return x * lax.rsqrt(jnp.maximum(_head_sum(x * x, seg), 1e-12))


def _f_norm_mod(x, g, sc, sh):
    return ((_rms(x) * g) * (1.0 + sc) + sh,)


def _f_rw_prep(pm, k_k, k_a, w0, wup, a0, aup, gup):
    k = pm[:, RW_W:2 * RW_W]
    kk = _l2n(k * k_k, RW_N)
    ws, kds, kas = [], [], []
    for d in range(2):
        wd = pm[:, 3 * RW_W + 128 * d:3 * RW_W + 128 * (d + 1)]
        ad = pm[:, 3 * RW_W + 256 + 128 * d:3 * RW_W + 256 + 128 * (d + 1)]
        wlog = -_softplus(-(w0[d:d + 1] + _bdot(jnp.tanh(wd), wup[128 * d:128 * (d + 1)]))) - 0.5
        ws.append(jnp.exp(-jnp.exp(wlog)))
        a = jax.nn.sigmoid(a0[d:d + 1] + _bdot(ad, aup[128 * d:128 * (d + 1)]))
        kds.append(k * (1.0 + (a - 1.0) * k_a))
        kas.append(kk * a)
    g = _bdot(jax.nn.sigmoid(pm[:, 3 * RW_W + 512:3 * RW_W + 512 + GATE_RANK]), gup)
    return kk, ws[0], ws[1], kds[0], kds[1], kas[0], kas[1], g


def _f_rw_read(o0, o1, r, kd0, kd1, v, g, r_k, gn_g, gn_b):
    o = o0 + o1
    c = o - _head_sum(o, RW_N) * (1.0 / RW_N)
    var = _head_sum(c * c, RW_N) * (1.0 / RW_N)
    on = c * lax.rsqrt(var + RW_GN_EPS) * gn_g + gn_b
    bonus = _head_sum(r * (kd0 + kd1) * r_k, RW_N) * v
    return ((on + bonus) * g,)


def _f_gd_prep(cq, ab, alog, dtb):
    qkv = cq * jax.nn.sigmoid(cq)
    q = _l2n(qkv[:, :GD_W], GD_N) * (GD_N ** -0.5)
    k = _l2n(qkv[:, GD_W:2 * GD_W], GD_N)
    v = qkv[:, 2 * GD_W:]
    glog = -jnp.exp(alog) * _softplus(ab + dtb)
    lane = lax.broadcasted_iota(jnp.int32, ab.shape, 1)
    return q, k, v, jnp.where(lane < 2 * GD_H, glog, jax.nn.sigmoid(ab))


def _f_gd_read(o0, o1, z, ng):
    o = o0 + o1
    y = o * lax.rsqrt(_head_sum(o * o, GD_N) * (1.0 / GD_N) + NORM_EPS) * jnp.concatenate([ng] * GD_H, axis=1)
    return (y * (z * jax.nn.sigmoid(z)),)


def _f_merge(za, zb, gates):
    return (jax.nn.sigmoid(gates[:, :D]) * za + jax.nn.sigmoid(gates[:, D:]) * zb,)


def _f_res_norm(x, att, g1, ng, sc, sh):
    x1 = x + g1 * att
    return x1, (_rms(x1) * ng) * (1.0 + sc) + sh


def _f_glu(gc, val):
    return (0.5 * gc * (1.0 + lax.erf(gc * (2.0 ** -0.5))) * val,)


def _f_final(x1, ff, target, g2, fg):
    y = _rms(x1 + g2 * ff) * fg
    err = y - target
    return (jnp.broadcast_to(0.5 * jnp.mean(err * err, axis=-1, keepdims=True), (x1.shape[0], 128)),)


def _exchange_copies(ins, outs, bcast, send_sems, recv_sems, local_sems):
    n = len(ins)
    x, y, c = lax.axis_index("x"), lax.axis_index("y"), lax.axis_index("c")
    me = 4 * x + 2 * y + c
    own = [pltpu.make_async_copy(ins[i] if bcast[i] else ins[i].at[me], outs[i].at[me], local_sems.at[i])
           for i in range(n)]
    sends, landings = [], []
    for k in range(1, NDEV):
        kx, ky, kc = (k >> 2) & 1, (k >> 1) & 1, k & 1
        px, py, pc = (1 - x if kx else x), (1 - y if ky else y), (1 - c if kc else c)
        peer = 4 * px + 2 * py + pc
        for i in range(n):
            sem = i * (NDEV - 1) + k - 1

            def copy(dst_block, i=i, sem=sem, peer=peer, dev=(px, py, pc)):
                return pltpu.make_async_remote_copy(
                    src_ref=ins[i] if bcast[i] else ins[i].at[peer], dst_ref=outs[i].at[dst_block],
                    send_sem=send_sems.at[sem], recv_sem=recv_sems.at[sem],
                    device_id=dev, device_id_type=pl.DeviceIdType.MESH)

            sends.append(copy(me))
            landings.append(copy(peer))

    def start():
        for cp in own + sends:
            cp.start()

    def wait():
        for cp, landing in zip(sends, landings):
            cp.wait_send()
            landing.wait_recv()
        for cp in own:
            cp.wait()

    return start, wait


def _exchange_sems(n):
    return [pltpu.SemaphoreType.DMA((n * (NDEV - 1),)), pltpu.SemaphoreType.DMA((n * (NDEV - 1),)),
            pltpu.SemaphoreType.DMA((n,))]


def _exchange_out_shapes(arrays, bcast):
    return [jax.ShapeDtypeStruct((NDEV,) + (a.shape if b else a.shape[1:]), a.dtype) for a, b in zip(arrays, bcast)]


def _exchange(arrays, bcast, name):
    n = len(arrays)

    def body(*refs):
        start, wait = _exchange_copies(refs[:n], refs[n:2 * n], bcast, *refs[2 * n:])
        start()
        wait()

    any_spec = pl.BlockSpec(memory_space=pl.ANY)
    return pl.pallas_call(
        body, in_specs=[any_spec] * n, out_specs=[any_spec] * n, out_shape=_exchange_out_shapes(arrays, bcast),
        scratch_shapes=_exchange_sems(n),
        compiler_params=pltpu.CompilerParams(has_side_effects=True), name=name)(*arrays)


def _adamw(slabs, w, m, v, name):
    R, C = w.shape
    ns = slabs.shape[0]
    tr = _pick(R, (256, 128, 64, 32, 16, 8))

    def body(s_ref, w_ref, m_ref, v_ref, g_ref, d_ref, nm_ref, nv_ref):
        g = s_ref[0].astype(F32)
        for i in range(1, ns):
            g = g + s_ref[i].astype(F32)
        nm = ADAM_B1 * m_ref[...] + (1.0 - ADAM_B1) * g
        nv = ADAM_B2 * v_ref[...] + (1.0 - ADAM_B2) * (g * g)
        m_hat = nm / (1.0 - ADAM_B1 ** ADAM_STEP)
        v_hat = nv / (1.0 - ADAM_B2 ** ADAM_STEP)
        g_ref[...] = g
        d_ref[...] = -ADAM_LR * (m_hat / (jnp.sqrt(v_hat) + ADAM_EPS) + ADAM_WD * w_ref[...])
        nm_ref[...] = nm
        nv_ref[...] = nv

    blk = pl.BlockSpec((tr, C), lambda i: (i, 0))
    out = jax.ShapeDtypeStruct((R, C), F32)
    return pl.pallas_call(
        body, grid=(R // tr,), in_specs=[pl.BlockSpec((ns, tr, C), lambda i: (0, i, 0)), blk, blk, blk],
        out_specs=[blk] * 4, out_shape=[out] * 4, compiler_params=_cparams(("parallel",)), name=name)(slabs, w, m, v)


def _silu(z):
    return z * jax.nn.sigmoid(z)


def _ada_forward(c_ext, w, b):
    n = w.shape[1]
    tn = _pick(n, (512, 256, 128))

    def body(c_ref, w_ref, b_ref, o_ref):
        o_ref[...] = _bdot(_silu(c_ref[...]), w_ref[...]) + b_ref[...]

    return pl.pallas_call(
        body, grid=(n // tn,),
        in_specs=[pl.BlockSpec(c_ext.shape, lambda j: (0, 0)), pl.BlockSpec((D, tn), lambda j: (0, j)),
                  pl.BlockSpec((1, tn), lambda j: (0, j))],
        out_specs=pl.BlockSpec((16, tn), lambda j: (0, j)), out_shape=jax.ShapeDtypeStruct((16, n), F32),
        compiler_params=_cparams(("parallel",)), name="ada_fwd")(c_ext, w, b)


def _ada_dmod(lat, ctxr):
    n = lat.shape[1]
    tn = 2048

    def body(lat_ref, ctx_ref, o_ref):
        o_ref[...] = jnp.concatenate([lat_ref[...], jnp.broadcast_to(jnp.sum(ctx_ref[...], axis=0, keepdims=True), (8, tn))], axis=0)

    blk = pl.BlockSpec((NDEV, tn), lambda j: (0, j))
    return pl.pallas_call(
        body, grid=(n // tn,), in_specs=[blk, blk],
        out_specs=pl.BlockSpec((16, tn), lambda j: (0, j)), out_shape=jax.ShapeDtypeStruct((16, n), F32),
        compiler_params=_cparams(("parallel",)), name="ada_dmod")(lat, ctxr)


def _f_colsum16(dm):
    return (jnp.broadcast_to(jnp.sum(dm, axis=0, keepdims=True), dm.shape),)


def _f_silu_grad(dpart, c_ext):
    sg = jax.nn.sigmoid(c_ext)
    return (dpart * sg * (1.0 + c_ext * (1.0 - sg)),)


def _f_silu(c_ext):
    return (_silu(c_ext),)


def _total_forward(rows):
    n = rows.shape[0]
    tm = _pick(n, (256, 128, 64, 32, 16, 8))

    def body(r_ref, o_ref):
        @pl.when(pl.program_id(0) == 0)
        def _():
            o_ref[...] = jnp.zeros_like(o_ref)

        o_ref[...] += jnp.broadcast_to(jnp.sum(r_ref[...], axis=0, keepdims=True), (8, 128))

    return pl.pallas_call(
        body, grid=(n // tm,), in_specs=[pl.BlockSpec((tm, 128), lambda i: (i, 0))],
        out_specs=pl.BlockSpec((8, 128), lambda i: (0, 0)), out_shape=jax.ShapeDtypeStruct((8, 128), F32),
        compiler_params=_cparams(("arbitrary",)), name="loss_total")(rows)


@jax.custom_vjp
def _total(rows):
    return _total_forward(rows)[0, 0]


def _total_fwd(rows):
    return _total_forward(rows)[0, 0], rows


def _total_bwd(rows, ct):
    lane = lax.broadcasted_iota(jnp.int32, rows.shape, 1)
    return (jnp.where(lane == 0, ct, 0.0).astype(F32),)


_total.defvjp(_total_fwd, _total_bwd)


def _pad_cols(a, width):
    return jnp.pad(a, ((0, 0), (0, width - a.shape[1])))


def _rw_cols(a):
    parts = [a[:, :3 * RW_W]]
    for i in range(4):
        parts.append(_pad_cols(a[:, 3 * RW_W + RANK * i:3 * RW_W + RANK * (i + 1)], 128))
    parts.append(a[:, 3 * RW_W + 4 * RANK:])
    return jnp.concatenate(parts, axis=1)


RW_COLS = 3 * RW_W + 4 * RANK + GATE_RANK
GD_COLS = 4 * GD_W + 4 * GD_H


def _split_w_in(w):
    gd = w[:, RW_COLS:RW_COLS + GD_COLS]
    rw = jnp.concatenate([_rw_cols(w[:, :RW_COLS]), _pad_cols(gd[:, 4 * GD_W:], 256)], axis=1)
    return w[:, RW_COLS + GD_COLS:], rw, gd[:, :4 * GD_W]


def _unshard(g, col):
    if col:
        return jnp.swapaxes(g, 0, 1).reshape(g.shape[1], NDEV * g.shape[2])
    return g.reshape(NDEV * g.shape[1], g.shape[2])


def _reshard(a, col):
    if col:
        return jnp.swapaxes(a.reshape(a.shape[0], NDEV, a.shape[1] // NDEV), 0, 1)
    return a.reshape(NDEV, a.shape[0] // NDEV, a.shape[1])


def _pack(parts, width, row_mult):
    flat = [p.reshape(-1) for p in parts]
    offs, o = [], 0
    for f in flat:
        offs.append(o)
        o += f.shape[0]
    rows = -(-o // (width * row_mult)) * row_mult
    cat = jnp.concatenate(flat + [jnp.zeros((rows * width - o,), flat[0].dtype)])
    return cat.reshape(rows, width), offs


BIG = ("w_in", "w_a_out", "w_b_out", "w_o", "ffn_w1", "ffn_w2")
LATE = BIG[1:]
SMALL = ("rw_w0", "rw_w_up", "rw_a0", "rw_a_up", "rw_g_up", "gd_conv_w", "ffn_conv_w")
ROW_SHARDED = ("w_o", "ffn_w2")
REPL = ("norm1_g", "norm2_g", "rw_mu", "rw_k_k", "rw_k_a", "rw_r_k", "rw_gn_g", "rw_gn_b", "gd_a_log", "gd_dt_bias",
        "gd_norm_g", "final_norm_g")
WEIGHTS = ('c_ctx', 'w_ada', 'b_ada', 'norm1_g', 'norm2_g', 'w_in', 'rw_mu', 'rw_k_k', 'rw_k_a', 'rw_r_k', 'rw_w0', 'rw_w_up', 'rw_a0', 'rw_a_up', 'rw_g_up', 'rw_gn_g', 'rw_gn_b', 'gd_conv_w', 'gd_a_log', 'gd_dt_bias', 'gd_norm_g', 'w_a_out', 'w_b_out', 'w_o', 'ffn_w1', 'ffn_conv_w', 'ffn_w2', 'final_norm_g')


def _view2d(a):
    return a.reshape(-1, a.shape[-1])


def _layer_loss(x, modl, modc, sinks, small, repl, ctx, target, big, n_ctx):
    sh1, sc1, g1, sh2, sc2, g2 = [modl[:, i * D:(i + 1) * D] for i in range(6)]
    csh1, csc1 = modc[:, :D], modc[:, D:2 * D]
    n1g, n2g, fg = repl["norm1_g"], repl["norm2_g"], repl["final_norm_g"].reshape(1, D)
    (h_lat,) = _rowwise(_f_norm_mod, "norm1_lat", tm=256)((x,), (n1g, sc1, sh1))
    (h_ctx,) = _rowwise(_f_norm_mod, "norm1_ctx", tm=256)((ctx,), (n1g, csc1, csh1))
    h = jnp.concatenate([h_ctx, h_lat], axis=0)
    wg, wr, wd = _split_w_in(big["w_in"])
    sg, sr, sd = _split_w_in(_unshard(sinks["w_in"], True))
    p_gate = _dense("proj_gate")(h_lat, wg, sg)
    p_rw = _dense("proj_rw")(h, wr, sr)
    p_gd = _dense("proj_gd")(h, wd, sd)
    n = h.shape[0]
    lat = slice(n_ctx, n)

    mu = _rw_cols(repl["rw_mu"])
    coef = jnp.concatenate([0.5 * mu, 1.0 - mu, 0.5 * mu], axis=0)
    pm = _stencil("rw_shift", ("seg", (n_ctx,)), ncols=RW_COLS + 128, tm=256, tc=768)(p_rw, coef)
    ab = p_rw[:, RW_COLS + 128:RW_COLS + 256]
    pad_rank = lambda a: jnp.pad(a, ((0, 0), (0, 128 - RANK), (0, 0))).reshape(256, RW_W)
    kk, w0, w1, kd0, kd1, ka0, ka1, g = _rowwise(_f_rw_prep, "rw_prep", tm=128)(
        (pm,), (repl["rw_k_k"], repl["rw_k_a"], small["rw_w0"], pad_rank(small["rw_w_up"]), small["rw_a0"],
                pad_rank(small["rw_a_up"]), small["rw_g_up"]))
    r, v = pm[:, :RW_W], pm[:, 2 * RW_W:3 * RW_W]
    o = _wkv_op(n_ctx)(r, v, kk, jnp.stack([w0, w1]), jnp.stack([kd0, kd1]), jnp.stack([ka0, ka1]))
    (ya,) = _rowwise(_f_rw_read, "rw_read", tm=256)(
        (o[0, lat], o[1, lat], r[lat], kd0[lat], kd1[lat], v[lat], g[lat]),
        (repl["rw_r_k"], repl["rw_gn_g"], repl["rw_gn_b"]))

    cq = _stencil("gd_conv", ("seg", (n_ctx,)), ncols=3 * GD_W, tm=256, tc=768)(p_gd, small["gd_conv_w"])
    z = p_gd[lat, 3 * GD_W:]
    lanes16 = lambda a: _pad_cols(a.reshape(1, 2 * GD_H), 128)
    q, k, v2, gb = _rowwise(_f_gd_prep, "gd_prep", tm=256)(
        (cq, ab), (lanes16(repl["gd_a_log"]), lanes16(repl["gd_dt_bias"])))
    per_head = lambda a: a.T.reshape(2, GD_H, n, 1)
    og, gathered, late_sinks = _gdn_op(n_ctx)(
        q, k, v2, per_head(gb[:, :2 * GD_H]), per_head(gb[:, 2 * GD_H:4 * GD_H]),
        tuple(big[k_] for k_ in LATE), tuple(sinks[k_] for k_ in LATE))
    (yb,) = _rowwise(_f_gd_read, "gd_read", tm=256)((og[0, lat], og[1, lat], z), (repl["gd_norm_g"],))
    w = {k_: _unshard(g_, k_ not in ROW_SHARDED) for k_, g_ in zip(LATE, gathered)}
    s = {k_: _unshard(s_, k_ not in ROW_SHARDED) for k_, s_ in zip(LATE, late_sinks)}

    za = _dense("a_out")(ya, w["w_a_out"], s["w_a_out"])
    zb = _dense("b_out")(yb, w["w_b_out"], s["w_b_out"])
    (merged,) = _rowwise(_f_merge, "merge", tm=256)((za, zb, p_gate), ())
    att = _dense("w_o")(merged, w["w_o"], s["w_o"])
    x1, h2 = _rowwise(_f_res_norm, "res_norm2", tm=256)((x, att), (g1, n2g, sc2, sh2))
    ug = _dense("ffn_gate")(h2, w["ffn_w1"][:, :D_FF], s["ffn_w1"][:, :D_FF])
    uv = _dense("ffn_val")(h2, w["ffn_w1"][:, D_FF:], s["ffn_w1"][:, D_FF:])
    gc = _stencil("ffn_conv", ("grid", GRID_W), ncols=D_FF, tm=256, tc=1408)(ug, small["ffn_conv_w"])
    (act,) = _rowwise(_f_glu, "glu", tm=64)((gc, uv), ())
    ff = _dense("ffn_w2")(act, w["ffn_w2"], s["ffn_w2"])
    (rows,) = _rowwise(_f_final, "final", tm=256, n_nondiff=1)((x1, ff, target), (g2, fg))
    return _total(rows)


def kernel(x, c, ctx, c_ctx, w_ada, b_ada, norm1_g, norm2_g, w_in, rw_mu, rw_k_k, rw_k_a, rw_r_k, rw_w0, rw_w_up, rw_a0, rw_a_up, rw_g_up, rw_gn_g, rw_gn_b, gd_conv_w, gd_a_log, gd_dt_bias, gd_norm_g, w_a_out, w_b_out, w_o, ffn_w1, ffn_conv_w, ffn_w2, final_norm_g, loss_target, m_c_ctx, m_w_ada, m_b_ada, m_norm1_g, m_norm2_g, m_w_in, m_rw_mu, m_rw_k_k, m_rw_k_a, m_rw_r_k, m_rw_w0, m_rw_w_up, m_rw_a0, m_rw_a_up, m_rw_g_up, m_rw_gn_g, m_rw_gn_b, m_gd_conv_w, m_gd_a_log, m_gd_dt_bias, m_gd_norm_g, m_w_a_out, m_w_b_out, m_w_o, m_ffn_w1, m_ffn_conv_w, m_ffn_w2, m_final_norm_g, v_c_ctx, v_w_ada, v_b_ada, v_norm1_g, v_norm2_g, v_w_in, v_rw_mu, v_rw_k_k, v_rw_k_a, v_rw_r_k, v_rw_w0, v_rw_w_up, v_rw_a0, v_rw_a_up, v_rw_g_up, v_rw_gn_g, v_rw_gn_b, v_gd_conv_w, v_gd_a_log, v_gd_dt_bias, v_gd_norm_g, v_w_a_out, v_w_b_out, v_w_o, v_ffn_w1, v_ffn_conv_w, v_ffn_w2, v_final_norm_g):
    args = dict(x=x, c=c, ctx=ctx, c_ctx=c_ctx, w_ada=w_ada, b_ada=b_ada, norm1_g=norm1_g, norm2_g=norm2_g, w_in=w_in, rw_mu=rw_mu, rw_k_k=rw_k_k, rw_k_a=rw_k_a, rw_r_k=rw_r_k, rw_w0=rw_w0, rw_w_up=rw_w_up, rw_a0=rw_a0, rw_a_up=rw_a_up, rw_g_up=rw_g_up, rw_gn_g=rw_gn_g, rw_gn_b=rw_gn_b, gd_conv_w=gd_conv_w, gd_a_log=gd_a_log, gd_dt_bias=gd_dt_bias, gd_norm_g=gd_norm_g, w_a_out=w_a_out, w_b_out=w_b_out, w_o=w_o, ffn_w1=ffn_w1, ffn_conv_w=ffn_conv_w, ffn_w2=ffn_w2, final_norm_g=final_norm_g, loss_target=loss_target, m_c_ctx=m_c_ctx, m_w_ada=m_w_ada, m_b_ada=m_b_ada, m_norm1_g=m_norm1_g, m_norm2_g=m_norm2_g, m_w_in=m_w_in, m_rw_mu=m_rw_mu, m_rw_k_k=m_rw_k_k, m_rw_k_a=m_rw_k_a, m_rw_r_k=m_rw_r_k, m_rw_w0=m_rw_w0, m_rw_w_up=m_rw_w_up, m_rw_a0=m_rw_a0, m_rw_a_up=m_rw_a_up, m_rw_g_up=m_rw_g_up, m_rw_gn_g=m_rw_gn_g, m_rw_gn_b=m_rw_gn_b, m_gd_conv_w=m_gd_conv_w, m_gd_a_log=m_gd_a_log, m_gd_dt_bias=m_gd_dt_bias, m_gd_norm_g=m_gd_norm_g, m_w_a_out=m_w_a_out, m_w_b_out=m_w_b_out, m_w_o=m_w_o, m_ffn_w1=m_ffn_w1, m_ffn_conv_w=m_ffn_conv_w, m_ffn_w2=m_ffn_w2, m_final_norm_g=m_final_norm_g, v_c_ctx=v_c_ctx, v_w_ada=v_w_ada, v_b_ada=v_b_ada, v_norm1_g=v_norm1_g, v_norm2_g=v_norm2_g, v_w_in=v_w_in, v_rw_mu=v_rw_mu, v_rw_k_k=v_rw_k_k, v_rw_k_a=v_rw_k_a, v_rw_r_k=v_rw_r_k, v_rw_w0=v_rw_w0, v_rw_w_up=v_rw_w_up, v_rw_a0=v_rw_a0, v_rw_a_up=v_rw_a_up, v_rw_g_up=v_rw_g_up, v_rw_gn_g=v_rw_gn_g, v_rw_gn_b=v_rw_gn_b, v_gd_conv_w=v_gd_conv_w, v_gd_a_log=v_gd_a_log, v_gd_dt_bias=v_gd_dt_bias, v_gd_norm_g=v_gd_norm_g, v_w_a_out=v_w_a_out, v_w_b_out=v_w_b_out, v_w_o=v_w_o, v_ffn_w1=v_ffn_w1, v_ffn_conv_w=v_ffn_conv_w, v_ffn_w2=v_ffn_w2, v_final_norm_g=v_final_norm_g)
    me = 4 * lax.axis_index("x") + 2 * lax.axis_index("y") + lax.axis_index("c")
    x2, ctx2, target = args["x"][0], args["ctx"][0], args["loss_target"][0]
    shard = {k: _view2d(args[k][0]) for k in BIG + SMALL}

    small_pack, small_offs = _pack([args["c"]] + [shard[k] for k in SMALL], 128, 8)
    gathered = _exchange([shard["w_in"].astype(BF16), small_pack], [True, True], "gather_weights")
    small_all = gathered[-1].reshape(NDEV, -1)

    def unpack(allg, off, k):
        r_, c_ = shard[k].shape
        return _unshard(allg[:, off:off + r_ * c_].reshape(NDEV, r_, c_), k not in ROW_SHARDED)

    big = {k: shard[k].astype(BF16) for k in LATE}
    big["w_in"] = _unshard(gathered[0], True)
    small = {k: unpack(small_all, o, k) for k, o in zip(SMALL, small_offs[1:])}
    small["rw_w_up"] = small["rw_w_up"].reshape(2, RANK, RW_W)
    small["rw_a_up"] = small["rw_a_up"].reshape(2, RANK, RW_W)
    c_all = small_all[:, :D]

    c_ext = jnp.concatenate([c_all, args["c_ctx"].reshape(1, D), jnp.zeros((7, D), F32)], axis=0)
    w_ada = args["w_ada"][0]
    nb = w_ada.shape[1]
    mod_blk = _ada_forward(c_ext, w_ada, lax.dynamic_slice(args["b_ada"], (0, me * nb), (1, nb)))
    (mod_all,) = _exchange([mod_blk], [True], "gather_mod")
    mod_all = jnp.swapaxes(mod_all, 0, 1).reshape(16, NDEV * nb)
    modl = lax.dynamic_slice(mod_all, (me, 0), (1, NDEV * nb))
    modc = mod_all[NDEV:NDEV + 1]

    repl = {k: args[k] for k in REPL}
    sinks = {k: jnp.zeros((NDEV,) + shard[k].shape, GRAD_WIRE) for k in BIG}
    n_ctx = ctx2.shape[0]
    loss, grads = jax.value_and_grad(_layer_loss, argnums=(0, 1, 2, 3, 4, 5))(
        x2, modl, modc, sinks, small, repl, ctx2, target, big, n_ctx)
    dx, dmodl, dmodc, dbig, dsmall, drepl = grads
    loss = lax.psum(loss, ("x", "y", "c"))

    rep_pack, rep_offs = _pack([dmodl, dmodc] + [drepl[k] for k in REPL], 128, 8)
    (rep_all,) = _exchange([rep_pack], [True], "gather_small_grads")
    rep_all = rep_all.reshape(NDEV, -1)
    six_d = NDEV * nb
    dmod_ext = _ada_dmod(rep_all[:, :six_d], rep_all[:, six_d:2 * six_d])
    dmod_blk = lax.dynamic_slice(dmod_ext, (0, me * nb), (16, nb))
    (s_ext,) = _rowwise(_f_silu, "ada_silu", tm=16)((c_ext,), ())
    g_w_ada = _mm(s_ext, dmod_blk, ta=True, name="ada_dw")
    dpart = _mm(dmod_blk, w_ada, tb=True, name="ada_dc")
    (dpart,) = _rowwise(_f_silu_grad, "ada_dsilu", tm=16)((dpart, c_ext), ())

    flat_small = [_reshard(_view2d(dsmall[k]) if k != "ffn_conv_w" else dsmall[k], True).reshape(NDEV, -1) for k in SMALL]
    small_send = jnp.concatenate(flat_small, axis=1)
    pad = -small_send.shape[1] % 1024
    small_send = jnp.pad(small_send, ((0, 0), (0, pad))).reshape(NDEV, -1, 128)
    got = _exchange([dbig["w_in"], small_send, dpart[NDEV:NDEV + 1]], [False, False, True], "scatter_grads")
    small_got = got[1].reshape(NDEV, -1)

    slabs = {k: dbig[k] for k in LATE}
    slabs["w_in"] = got[0]
    o = 0
    for k in SMALL:
        r_, c_ = shard[k].shape
        slabs[k] = small_got[:, o:o + r_ * c_].reshape(NDEV, r_, c_)
        o += r_ * c_
    slabs["w_ada"] = g_w_ada[None]
    slabs["c_ctx"] = got[2]
    slabs["b_ada"] = jnp.concatenate([rep_all[:, :six_d], rep_all[:, six_d:2 * six_d]], axis=0)[:, None, :]
    for k, off in zip(REPL, rep_offs[2:]):
        size = args[k].size
        slabs[k] = rep_all[:, off:off + size].reshape((NDEV,) + _view2d(args[k]).shape)

    outs = {}
    for k in WEIGHTS:
        shape = args[k].shape
        res = _adamw(slabs[k], _view2d(args[k]), _view2d(args["m_" + k]), _view2d(args["v_" + k]), "adamw_" + k)
        outs[k] = [a.reshape(shape) for a in res]
    return (loss, dx[None]) + tuple(outs[k][i] for i in range(4) for k in WEIGHTS)
```

```python
import functools
import math

import jax
import jax.numpy as jnp
from jax import lax
from jax.experimental import pallas as pl
from jax.experimental.pallas import tpu as pltpu

F32 = jnp.float32
BF16 = jnp.bfloat16
HI = lax.Precision.HIGHEST
GRAD_WIRE = BF16

D = 2048
NCTX = 256
GRID_W = 64
NORM_EPS = 1e-6
RW_H, RW_N = 16, 64
RW_W = RW_H * RW_N
RANK = 96
GATE_RANK = 256
RW_GN_EPS = 64e-5
GD_H, GD_N = 8, 128
GD_W = GD_H * GD_N
CHUNK = 64
D_FF = 5632
NDEV = 8
PW = 4096
VMEM_LIMIT = 56 * 1024 * 1024
MM_VMEM_BUDGET = 46 * 1024 * 1024

ADAM_LR, ADAM_B1, ADAM_B2, ADAM_EPS, ADAM_WD, ADAM_STEP = 0.001, 0.9, 0.999, 1e-08, 0.01, 10


def _cparams(sem):
    return pltpu.CompilerParams(dimension_semantics=sem, vmem_limit_bytes=VMEM_LIMIT)


def _pick(n, cands):
    for c in cands:
        if n % c == 0:
            return c
    return n


def _mm(a, b, *, ta=False, tb=False, name, out_dtype=F32):
    M = a.shape[1] if ta else a.shape[0]
    K = a.shape[0] if ta else a.shape[1]
    N = b.shape[0] if tb else b.shape[1]
    assert K == (b.shape[1] if tb else b.shape[0])
    tm = _pick(M, (512, 768, 256, 128, 64, 16, 8))

    def vmem_bytes(tn_):
        return (2 * tm * K * a.dtype.itemsize + tm * K * 2 + 2 * tn_ * K * b.dtype.itemsize
                + 2 * tm * tn_ * jnp.dtype(out_dtype).itemsize)

    tn = next(t for t in (1024, 512, 256, 128) if N % t == 0 and (vmem_bytes(t) <= MM_VMEM_BUDGET or t == 128))
    dn = (((0 if ta else 1,), (1 if tb else 0,)), ((), ()))

    def body(a_ref, b_ref, o_ref, a16):
        @pl.when(pl.program_id(1) == 0)
        def _():
            a16[...] = a_ref[...].astype(BF16)

        o_ref[...] = lax.dot_general(a16[...], b_ref[...].astype(BF16), dn,
                                     preferred_element_type=F32).astype(out_dtype)

    a_blk = (K, tm) if ta else (tm, K)
    a_spec = pl.BlockSpec(a_blk, (lambda i, j: (0, i)) if ta else (lambda i, j: (i, 0)))
    b_spec = pl.BlockSpec((tn, K), lambda i, j: (j, 0)) if tb else pl.BlockSpec((K, tn), lambda i, j: (0, j))
    return pl.pallas_call(
        body, grid=(M // tm, N // tn), in_specs=[a_spec, b_spec],
        out_specs=pl.BlockSpec((tm, tn), lambda i, j: (i, j)),
        out_shape=jax.ShapeDtypeStruct((M, N), out_dtype),
        scratch_shapes=[pltpu.VMEM(a_blk, BF16)],
        compiler_params=_cparams(("parallel", "arbitrary")), name=name)(a, b)


def _dense(name):
    @jax.custom_vjp
    def f(a, w, sink):
        return _mm(a, w, name=name + "_fwd")

    def fwd(a, w, sink):
        return _mm(a, w, name=name + "_fwd"), (a, w)

    def bwd(res, dc):
        a, w = res
        return (_mm(dc, w, tb=True, name=name + "_da"), None,
                _mm(a, dc, ta=True, name=name + "_dw", out_dtype=GRAD_WIRE))

    f.defvjp(fwd, bwd)
    return f


def _rowwise(fn, name, *, tm, n_nondiff=0):
    def shapes(rows, params):
        tiles = [jax.ShapeDtypeStruct((tm, r.shape[1]), r.dtype) for r in rows]
        ps = [jax.ShapeDtypeStruct(p.shape, p.dtype) for p in params]
        return tiles, ps, jax.eval_shape(fn, *tiles, *ps)

    def row_spec(width):
        return pl.BlockSpec((tm, width), lambda i: (i, 0))

    def whole_spec(shape):
        return pl.BlockSpec(shape, lambda i: (0,) * len(shape))

    def forward(rows, params):
        n = rows[0].shape[0]
        tiles, ps, outs = shapes(rows, params)
        nin = len(rows) + len(params)

        def body(*refs):
            res = fn(*[r[...] for r in refs[:nin]])
            for o_ref, v in zip(refs[nin:], res):
                o_ref[...] = v.astype(o_ref.dtype)

        return pl.pallas_call(
            body, grid=(n // tm,),
            in_specs=[row_spec(t.shape[1]) for t in tiles] + [whole_spec(p.shape) for p in ps],
            out_specs=[row_spec(o.shape[1]) for o in outs],
            out_shape=[jax.ShapeDtypeStruct((n, o.shape[1]), o.dtype) for o in outs],
            compiler_params=_cparams(("parallel",)), name=name + "_fwd")(*rows, *params)

    def backward(rows, params, cts):
        n = rows[0].shape[0]
        tiles, ps, outs = shapes(rows, params)
        nr, npar, nout = len(rows), len(params), len(outs)
        nd = nr - n_nondiff

        def body(*refs):
            ins = [r[...] for r in refs[:nr + npar]]
            ct = tuple(r[...] for r in refs[nr + npar:nr + npar + nout])
            out_refs = refs[nr + npar + nout:]
            fixed = ins[nd:nr]

            def g(*diff):
                return fn(*diff[:nd], *fixed, *diff[nd:])

            _, vjp = jax.vjp(g, *ins[:nd], *ins[nr:])
            grads = vjp(ct)
            for o_ref, v in zip(out_refs[:nd], grads[:nd]):
                o_ref[...] = v.astype(o_ref.dtype)

            @pl.when(pl.program_id(0) == 0)
            def _():
                for o_ref in out_refs[nd:]:
                    o_ref[...] = jnp.zeros_like(o_ref)

            for o_ref, v in zip(out_refs[nd:], grads[nd:]):
                o_ref[...] += v

        res = pl.pallas_call(
            body, grid=(n // tm,),
            in_specs=[row_spec(t.shape[1]) for t in tiles] + [whole_spec(p.shape) for p in ps]
            + [row_spec(o.shape[1]) for o in outs],
            out_specs=[row_spec(t.shape[1]) for t in tiles[:nd]] + [whole_spec(p.shape) for p in ps],
            out_shape=[jax.ShapeDtypeStruct((n, t.shape[1]), t.dtype) for t in tiles[:nd]]
            + [jax.ShapeDtypeStruct(p.shape, p.dtype) for p in ps],
            compiler_params=_cparams(("arbitrary",)), name=name + "_bwd")(*rows, *params, *cts)
        return tuple(res[:nd]) + (None,) * n_nondiff, tuple(res[nd:])

    @jax.custom_vjp
    def op(rows, params):
        return tuple(forward(rows, params))

    def op_fwd(rows, params):
        return tuple(forward(rows, params)), (rows, params)

    def op_bwd(res, cts):
        return backward(res[0], res[1], cts)

    op.defvjp(op_fwd, op_bwd)
    return op


def _valid(t, off, n, mode):
    s = t + off[0]
    ok = (s >= 0) & (s < n)
    if mode[0] == "seg":
        for b in mode[1]:
            ok = ok & ((t >= b) == (s >= b))
    else:
        col = lax.rem(t, mode[1]) + off[1]
        ok = ok & (col >= 0) & (col < mode[1])
    return ok


def _stencil_offsets(mode):
    if mode[0] == "seg":
        return [(-1, 0), (0, 0), (1, 0)]
    w = mode[1]
    return [(di * w + dj, dj) for di in (-1, 0, 1) for dj in (-1, 0, 1)]


def _stencil(name, mode, *, ncols, tm, tc):
    offs = _stencil_offsets(mode)
    J = len(offs)
    halo = 8 if mode[0] == "seg" else 128

    def x_specs(n):
        nb = n // halo
        r = tm // halo
        return [pl.BlockSpec((halo, tc), lambda i, j: (jnp.maximum(i * r - 1, 0), j)),
                pl.BlockSpec((tm, tc), lambda i, j: (i, j)),
                pl.BlockSpec((halo, tc), lambda i, j: (jnp.minimum((i + 1) * r, nb - 1), j))]

    def fill(buf, prev, cur, nxt):
        buf[0:halo, :] = prev[...]
        buf[halo:halo + tm, :] = cur[...]
        buf[halo + tm:, :] = nxt[...]

    def forward(x, coef, suffix=""):
        n = x.shape[0]

        def body(prev, cur, nxt, c_ref, o_ref, buf):
            fill(buf, prev, cur, nxt)
            t = pl.program_id(0) * tm + lax.broadcasted_iota(jnp.int32, (tm, 1), 0)
            acc = jnp.zeros((tm, tc), F32)
            for j, off in enumerate(offs):
                xs = buf[halo + off[0]:halo + off[0] + tm, :]
                acc = acc + jnp.where(_valid(t, off, n, mode), xs, 0.0) * c_ref[j:j + 1, :]
            o_ref[...] = acc

        return pl.pallas_call(
            body, grid=(n // tm, ncols // tc),
            in_specs=x_specs(n) + [pl.BlockSpec((J, tc), lambda i, j: (0, j))],
            out_specs=pl.BlockSpec((tm, tc), lambda i, j: (i, j)),
            out_shape=jax.ShapeDtypeStruct((n, ncols), F32),
            scratch_shapes=[pltpu.VMEM((tm + 2 * halo, tc), F32)],
            compiler_params=_cparams(("parallel", "parallel")), name=name + suffix)(x, x, x, coef)

    def wgrad(x, dy):
        n = x.shape[0]

        def body(prev, cur, nxt, dy_ref, o_ref, buf):
            fill(buf, prev, cur, nxt)
            i = pl.program_id(1)
            t = i * tm + lax.broadcasted_iota(jnp.int32, (tm, 1), 0)

            @pl.when(i == 0)
            def _():
                o_ref[...] = jnp.zeros_like(o_ref)

            dy = dy_ref[...]
            for j, off in enumerate(offs):
                xs = buf[halo + off[0]:halo + off[0] + tm, :]
                o_ref[j:j + 1, :] += jnp.sum(jnp.where(_valid(t, off, n, mode), xs, 0.0) * dy, axis=0, keepdims=True)

        specs = [pl.BlockSpec(s.block_shape, (lambda f: lambda j, i: f(i, j))(s.index_map)) for s in x_specs(n)]
        return pl.pallas_call(
            body, grid=(ncols // tc, n // tm),
            in_specs=specs + [pl.BlockSpec((tm, tc), lambda j, i: (i, j))],
            out_specs=pl.BlockSpec((J, tc), lambda j, i: (0, j)),
            out_shape=jax.ShapeDtypeStruct((J, ncols), F32),
            scratch_shapes=[pltpu.VMEM((tm + 2 * halo, tc), F32)],
            compiler_params=_cparams(("parallel", "arbitrary")), name=name + "_wgrad")(x, x, x, dy)

    @jax.custom_vjp
    def op(x, coef):
        return forward(x, coef)

    def op_fwd(x, coef):
        return forward(x, coef), (x, coef)

    def op_bwd(res, dy):
        x, coef = res
        dx = forward(dy, coef[::-1], "_adj")
        if x.shape[1] != ncols:
            dx = jnp.pad(dx, ((0, 0), (0, x.shape[1] - ncols)))
        return dx, wgrad(x, dy)

    op.defvjp(op_fwd, op_bwd)
    return op


WKV_TC = 32
WKV_TB = 64


def _chunk_index(d, c, n_ctx_chunks, n_chunks):
    rev = jnp.where(c < n_ctx_chunks, n_ctx_chunks - 1 - c, n_ctx_chunks + n_chunks - 1 - c)
    return jnp.where(d == 0, c, rev)


def _to_feature_major(a):
    n = a.shape[0]
    t = a.reshape(n // WKV_TB, WKV_TB, RW_H // 2, 2, RW_N).transpose(2, 4, 0, 3, 1)
    return t.reshape(RW_W // 2, n // WKV_TB * 128)


def _from_feature_major(t):
    n = t.shape[1] // 128 * WKV_TB
    a = t.reshape(RW_H // 2, RW_N, n // WKV_TB, 2, WKV_TB).transpose(2, 4, 0, 3, 1)
    return a.reshape(n, RW_W)


def _pair_consts():
    lane = lax.broadcasted_iota(jnp.int32, (1, 128), 1)
    first = lax.broadcasted_iota(jnp.int32, (RW_N, 128), 1) < RW_N
    same = (lax.div(lax.broadcasted_iota(jnp.int32, (128, 128), 0), RW_N)
            == lax.div(lax.broadcasted_iota(jnp.int32, (128, 128), 1), RW_N))
    return lane < RW_N, lane >= RW_N, first, same.astype(BF16)


def _seg_reduce(x, row, m0, m1, first):
    s0 = jnp.sum(x * jnp.where(m0, row, 0.0), axis=1, keepdims=True)
    s1 = jnp.sum(x * jnp.where(m1, row, 0.0), axis=1, keepdims=True)
    return jnp.where(first, s0, s1)


def _seg_sum_mxu(x, same, passes):
    acc, rest = None, x
    for i in range(passes):
        piece = rest.astype(BF16)
        part = jnp.dot(piece, same, preferred_element_type=F32)
        acc = part if acc is None else acc + part
        if i + 1 < passes:
            rest = rest - piece.astype(F32)
    return acc


def _wkv_forward(r, vT, kk, w, kd, ka, n_ctx):
    n = r.shape[0]
    nc, ncc = n // WKV_TC, n_ctx // WKV_TC
    per_blk = WKV_TB // WKV_TC

    def body(r_ref, vT_ref, kk_ref, w_ref, kd_ref, ka_ref, oT_ref, st_ref, S_ref):
        d, c = pl.program_id(0), pl.program_id(1)
        ci = _chunk_index(d, c, ncc, nc)

        @pl.when(c == 0)
        def _():
            S_ref[...] = jnp.zeros_like(S_ref)

        @pl.when(lax.rem(c, per_blk) == 0)
        def _():
            oT_ref[...] = jnp.zeros_like(oT_ref)

        m0, m1, first, same = _pair_consts()
        lane_t = lax.rem(lax.broadcasted_iota(jnp.int32, (RW_N, 128), 1), WKV_TB)

        def step(s, carry):
            tl = jnp.where(d == 0, s, WKV_TC - 1 - s)
            sel = lane_t == lax.rem(ci, per_blk) * WKV_TC + tl
            row = pl.ds(tl, 1)
            kn_all, w_all, kd_all, ka_all, r_all = -kk_ref[row, :], w_ref[row, :], kd_ref[row, :], ka_ref[row, :], r_ref[row, :]
            pairs = range(RW_H // 2)
            cs = [slice(p * 128, (p + 1) * 128) for p in pairs]
            rows = [slice(p * RW_N, (p + 1) * RW_N) for p in pairs]
            S = [S_ref[p] for p in pairs]
            for p in pairs:
                st_ref[tl, p] = S[p]
            sab = [_seg_reduce(S[p], kn_all[:, cs[p]], m0, m1, first) for p in pairs]
            vb = [_seg_sum_mxu(jnp.where(sel, vT_ref[rows[p], :], 0.0), same, 2) for p in pairs]
            S = [S[p] * w_all[:, cs[p]] + sab[p] * ka_all[:, cs[p]] + vb[p] * kd_all[:, cs[p]] for p in pairs]
            for p in pairs:
                S_ref[p] = S[p]
            for p in pairs:
                ob = _seg_reduce(S[p], r_all[:, cs[p]], m0, m1, first)
                oT_ref[rows[p], :] = jnp.where(sel, ob, oT_ref[rows[p], :])
            return carry

        lax.fori_loop(0, WKV_TC, step, 0)

    def row(dirn):
        if dirn:
            return pl.BlockSpec((None, WKV_TC, RW_W), lambda d, c: (d, _chunk_index(d, c, ncc, nc), 0))
        return pl.BlockSpec((WKV_TC, RW_W), lambda d, c: (_chunk_index(d, c, ncc, nc), 0))

    colT = pl.BlockSpec((RW_W // 2, 128), lambda d, c: (0, lax.div(_chunk_index(d, c, ncc, nc), per_blk)))
    return pl.pallas_call(
        body, grid=(2, nc),
        in_specs=[row(0), colT, row(0), row(1), row(1), row(1)],
        out_specs=[pl.BlockSpec((None, RW_W // 2, 128), lambda d, c: (d, 0, lax.div(_chunk_index(d, c, ncc, nc), per_blk))),
                   pl.BlockSpec((None, WKV_TC, RW_H // 2, RW_N, 128),
                                lambda d, c: (d, _chunk_index(d, c, ncc, nc), 0, 0, 0))],
        out_shape=[jax.ShapeDtypeStruct((2, RW_W // 2, n // WKV_TB * 128), F32),
                   jax.ShapeDtypeStruct((2, n, RW_H // 2, RW_N, 128), F32)],
        scratch_shapes=[pltpu.VMEM((RW_H // 2, RW_N, 128), F32)],
        compiler_params=_cparams(("arbitrary", "arbitrary")), name="wkv_fwd")(r, vT, kk, w, kd, ka)


def _wkv_backward(r, vT, kk, w, kd, ka, st, doT, n_ctx):
    n = r.shape[0]
    nc, ncc = n // WKV_TC, n_ctx // WKV_TC
    per_blk = WKV_TB // WKV_TC

    def body(r_ref, vT_ref, kk_ref, w_ref, kd_ref, ka_ref, st_ref, doT_ref,
             dr_ref, dw_ref, dkd_ref, dkn_ref, dka_ref, dvT_ref, dS_ref):
        d, c = pl.program_id(0), pl.program_id(1)
        ci = _chunk_index(d, nc - 1 - c, ncc, nc)

        @pl.when(c == 0)
        def _():
            dS_ref[...] = jnp.zeros_like(dS_ref)

        @pl.when(lax.rem(c, per_blk) == 0)
        def _():
            dvT_ref[...] = jnp.zeros_like(dvT_ref)

        m0, m1, first, same = _pair_consts()
        lane_t = lax.rem(lax.broadcasted_iota(jnp.int32, (RW_N, 128), 1), WKV_TB)

        def step(s, carry):
            tl = jnp.where(d == 0, WKV_TC - 1 - s, s)
            sel = lane_t == lax.rem(ci, per_blk) * WKV_TC + tl
            row = pl.ds(tl, 1)
            kn_all, w_all, kd_all, ka_all, r_all = -kk_ref[row, :], w_ref[row, :], kd_ref[row, :], ka_ref[row, :], r_ref[row, :]
            pairs = range(RW_H // 2)
            cs = [slice(p * 128, (p + 1) * 128) for p in pairs]
            rows = [slice(p * RW_N, (p + 1) * RW_N) for p in pairs]
            colsum = lambda a: jnp.sum(a, axis=0, keepdims=True)
            Sp = [st_ref[tl, p] for p in pairs]
            dob = [_seg_sum_mxu(jnp.where(sel, doT_ref[rows[p], :], 0.0), same, 2) for p in pairs]
            dS = [dS_ref[p] + dob[p] * r_all[:, cs[p]] for p in pairs]
            dsab = [_seg_reduce(dS[p], ka_all[:, cs[p]], m0, m1, first) for p in pairs]
            sab = [_seg_reduce(Sp[p], kn_all[:, cs[p]], m0, m1, first) for p in pairs]
            vb = [_seg_sum_mxu(jnp.where(sel, vT_ref[rows[p], :], 0.0), same, 2) for p in pairs]
            St = [Sp[p] * w_all[:, cs[p]] + sab[p] * ka_all[:, cs[p]] + vb[p] * kd_all[:, cs[p]] for p in pairs]
            dr_ref[row, :] = jnp.concatenate([colsum(St[p] * dob[p]) for p in pairs], axis=1)
            dw_ref[row, :] = jnp.concatenate([colsum(dS[p] * Sp[p]) for p in pairs], axis=1)
            dka_ref[row, :] = jnp.concatenate([colsum(dS[p] * sab[p]) for p in pairs], axis=1)
            dkd_ref[row, :] = jnp.concatenate([colsum(dS[p] * vb[p]) for p in pairs], axis=1)
            dkn_ref[row, :] = jnp.concatenate([colsum(Sp[p] * dsab[p]) for p in pairs], axis=1)
            for p in pairs:
                dvb = _seg_reduce(dS[p], kd_all[:, cs[p]], m0, m1, first)
                dvT_ref[rows[p], :] = jnp.where(sel, dvb, dvT_ref[rows[p], :])
                dS_ref[p] = dS[p] * w_all[:, cs[p]] + dsab[p] * kn_all[:, cs[p]]
            return carry

        lax.fori_loop(0, WKV_TC, step, 0)

    def cidx(d, c):
        return _chunk_index(d, nc - 1 - c, ncc, nc)

    def row(dirn):
        if dirn:
            return pl.BlockSpec((None, WKV_TC, RW_W), lambda d, c: (d, cidx(d, c), 0))
        return pl.BlockSpec((WKV_TC, RW_W), lambda d, c: (cidx(d, c), 0))

    colT = pl.BlockSpec((RW_W // 2, 128), lambda d, c: (0, lax.div(cidx(d, c), per_blk)))
    colT_d = pl.BlockSpec((None, RW_W // 2, 128), lambda d, c: (d, 0, lax.div(cidx(d, c), per_blk)))
    st_spec = pl.BlockSpec((None, WKV_TC, RW_H // 2, RW_N, 128), lambda d, c: (d, cidx(d, c), 0, 0, 0))
    rows_out = jax.ShapeDtypeStruct((2, n, RW_W), F32)
    return pl.pallas_call(
        body, grid=(2, nc),
        in_specs=[row(0), colT, row(0), row(1), row(1), row(1), st_spec, colT_d],
        out_specs=[row(1)] * 5 + [colT_d],
        out_shape=[rows_out] * 5 + [jax.ShapeDtypeStruct((2, RW_W // 2, n // WKV_TB * 128), F32)],
        scratch_shapes=[pltpu.VMEM((RW_H // 2, RW_N, 128), F32)],
        compiler_params=_cparams(("arbitrary", "arbitrary")), name="wkv_bwd")(r, vT, kk, w, kd, ka, st, doT)


def _wkv_op(n_ctx):
    def readout(oT):
        return jnp.stack([_from_feature_major(oT[0]), _from_feature_major(oT[1])])

    @jax.custom_vjp
    def op(r, v, kk, w, kd, ka):
        return readout(_wkv_forward(r, _to_feature_major(v), kk, w, kd, ka, n_ctx)[0])

    def fwd(r, v, kk, w, kd, ka):
        vT = _to_feature_major(v)
        oT, st = _wkv_forward(r, vT, kk, w, kd, ka, n_ctx)
        return readout(oT), (r, vT, kk, w, kd, ka, st)

    def bwd(res, do):
        r, vT, kk, w, kd, ka, st = res
        doT = jnp.stack([_to_feature_major(do[0]), _to_feature_major(do[1])])
        dr, dw, dkd, dkn, dka, dvT = _wkv_backward(r, vT, kk, w, kd, ka, st, doT, n_ctx)
        return dr[0] + dr[1], _from_feature_major(dvT[0] + dvT[1]), -(dkn[0] + dkn[1]), dw, dkd, dka

    op.defvjp(fwd, bwd)
    return op


GD_HB = 2


def _dot(a, b, dims=((1,), (0,))):
    return lax.dot_general(a, b, (dims, ((), ())), precision=HI, preferred_element_type=F32)


def _gdn_chunk(S, q, k, v, g, beta, rev):
    R = q.shape[0]
    C = CHUNK
    nh = R // C
    ri = lax.broadcasted_iota(jnp.int32, (R, R), 0)
    cj = lax.broadcasted_iota(jnp.int32, (R, R), 1)
    same = lax.div(ri, C) == lax.div(cj, C)
    lag = (ri - cj) * (1 - 2 * rev.astype(jnp.int32))
    incl = same & (lag >= 0)
    strict = same & (lag > 0)
    eye = (ri == cj).astype(F32)
    gb = jnp.broadcast_to(g, (R, GD_N))
    bb = jnp.broadcast_to(beta, (R, GD_N))
    G = _dot(incl.astype(F32), gb)
    Gc = jnp.concatenate([G] * (R // GD_N), axis=1) if R > GD_N else G[:, :R]
    Grow = _dot(jnp.ones((R, R), F32), eye * Gc)
    decay = jnp.where(incl, jnp.exp(jnp.where(incl, Gc - Grow, 0.0)), 0.0)
    kb = k * bb
    A = jnp.where(strict, _dot(kb, k, ((1,), (1,))) * decay, 0.0)
    Nk = -A
    T = eye + Nk
    for _ in range(int(math.log2(C)) - 1):
        Nk = _dot(Nk, Nk)
        T = T + _dot(T, Nk)
    u = _dot(T, v * bb)
    w = _dot(T, kb * jnp.exp(G))
    attn = jnp.where(incl, _dot(q, k, ((1,), (1,))) * decay, 0.0)
    head = lax.div(lax.broadcasted_iota(jnp.int32, (R, GD_N), 0), C)

    def spread(a):
        return jnp.concatenate([jnp.where(head == h, a, 0.0) for h in range(nh)], axis=1)

    v_new = u - _dot(spread(w), S)
    o = _dot(spread(q * jnp.exp(G)), S) + _dot(attn, v_new)
    Gtot = _dot(same.astype(F32), gb)
    k_dec = k * jnp.exp(Gtot - G)
    Gs = jnp.concatenate([Gtot[h * C:(h + 1) * C] for h in range(nh) for _ in range(GD_N // C)], axis=0)
    S_new = S * jnp.exp(Gs) + _dot(spread(k_dec), v_new, ((0,), (0,)))
    return S_new, o


def _stack_heads(ref):
    return jnp.concatenate([ref[:, h * GD_N:(h + 1) * GD_N] for h in range(GD_HB)], axis=0)


def _unstack_heads(ref, a):
    for h in range(GD_HB):
        ref[:, h * GD_N:(h + 1) * GD_N] = a[h * CHUNK:(h + 1) * CHUNK]


def _gdn_specs(n, n_ctx, back):
    nc, ncc = n // CHUNK, n_ctx // CHUNK

    def ci(d, c):
        return _chunk_index(d, nc - 1 - c if back else c, ncc, nc)

    tok = pl.BlockSpec((CHUNK, GD_HB * GD_N), lambda d, h, c: (ci(d, c), h))
    tok_d = pl.BlockSpec((None, CHUNK, GD_HB * GD_N), lambda d, h, c: (d, ci(d, c), h))
    col = pl.BlockSpec((None, GD_HB, CHUNK, 1), lambda d, h, c: (d, h, ci(d, c), 0))
    st = pl.BlockSpec((None, GD_HB, None, GD_N, GD_N), lambda d, h, c: (d, h, ci(d, c), 0, 0))
    return nc, tok, tok_d, col, st


def _ride(ride, bcast, first, last, refs_in, refs_out, sems):
    if not ride:
        return
    start, wait = _exchange_copies(refs_in, refs_out, [bcast] * len(ride), *sems)
    pl.when(first)(start)
    pl.when(last)(wait)


def _grid_ends(nc):
    d, h, c = pl.program_id(0), pl.program_id(1), pl.program_id(2)
    first = (d == 0) & (h == 0) & (c == 0)
    last = (d == 1) & (h == GD_H // GD_HB - 1) & (c == nc - 1)
    return first, last


def _gdn_forward(q, k, v, g, beta, n_ctx, ride=()):
    n = q.shape[0]
    nc, tok, tok_d, col, st = _gdn_specs(n, n_ctx, False)
    nr = len(ride)

    def body(*refs):
        q_ref, k_ref, v_ref, g_ref, b_ref = refs[:5]
        o_ref, st_ref = refs[5 + nr:7 + nr]
        S_ref = refs[7 + 2 * nr]
        first, last = _grid_ends(nc)
        _ride(ride, True, first, last, refs[5:5 + nr], refs[7 + nr:7 + 2 * nr], refs[8 + 2 * nr:])

        @pl.when(pl.program_id(2) == 0)
        def _():
            S_ref[...] = jnp.zeros_like(S_ref)

        S = S_ref[...]
        st_ref[...] = S.reshape(GD_HB, GD_N, GD_N)
        S_new, o = _gdn_chunk(S, _stack_heads(q_ref), _stack_heads(k_ref), _stack_heads(v_ref),
                              g_ref[...].reshape(GD_HB * CHUNK, 1), b_ref[...].reshape(GD_HB * CHUNK, 1),
                              pl.program_id(0) == 1)
        S_ref[...] = S_new
        _unstack_heads(o_ref, o)

    any_spec = pl.BlockSpec(memory_space=pl.ANY)
    res = pl.pallas_call(
        body, grid=(2, GD_H // GD_HB, nc), in_specs=[tok, tok, tok, col, col] + [any_spec] * nr,
        out_specs=[tok_d, st] + [any_spec] * nr,
        out_shape=[jax.ShapeDtypeStruct((2, n, GD_W), F32), jax.ShapeDtypeStruct((2, GD_H, nc, GD_N, GD_N), F32)]
        + _exchange_out_shapes(ride, [True] * nr),
        scratch_shapes=[pltpu.VMEM((GD_HB * GD_N, GD_N), F32)] + (_exchange_sems(nr) if nr else []),
        compiler_params=_cparams(("arbitrary", "arbitrary", "arbitrary")), name="gdn_fwd")(q, k, v, g, beta, *ride)
    return res[0], res[1], tuple(res[2:])


def _gdn_backward(q, k, v, g, beta, st, do, n_ctx, ride=()):
    n = q.shape[0]
    nc, tok, tok_d, col, st_spec = _gdn_specs(n, n_ctx, True)
    nr = len(ride)

    def body(*refs):
        q_ref, k_ref, v_ref, g_ref, b_ref, st_ref, do_ref = refs[:7]
        dq_ref, dk_ref, dv_ref, dg_ref, db_ref = refs[7 + nr:12 + nr]
        dS_ref = refs[12 + 2 * nr]
        first, last = _grid_ends(nc)
        _ride(ride, False, first, last, refs[7:7 + nr], refs[12 + nr:12 + 2 * nr], refs[13 + 2 * nr:])

        @pl.when(pl.program_id(2) == 0)
        def _():
            dS_ref[...] = jnp.zeros_like(dS_ref)

        rev = pl.program_id(0) == 1
        rows = GD_HB * CHUNK
        _, vjp = jax.vjp(lambda S, q_, k_, v_, g_, b_: _gdn_chunk(S, q_, k_, v_, g_, b_, rev),
                         st_ref[...].reshape(GD_HB * GD_N, GD_N), _stack_heads(q_ref), _stack_heads(k_ref),
                         _stack_heads(v_ref), g_ref[...].reshape(rows, 1), b_ref[...].reshape(rows, 1))
        dS, dq, dk, dv, dg, db = vjp((dS_ref[...], _stack_heads(do_ref)))
        dS_ref[...] = dS
        _unstack_heads(dq_ref, dq)
        _unstack_heads(dk_ref, dk)
        _unstack_heads(dv_ref, dv)
        dg_ref[...] = dg.reshape(GD_HB, CHUNK, 1)
        db_ref[...] = db.reshape(GD_HB, CHUNK, 1)

    tok_out = jax.ShapeDtypeStruct((2, n, GD_W), F32)
    col_out = jax.ShapeDtypeStruct((2, GD_H, n, 1), F32)
    any_spec = pl.BlockSpec(memory_space=pl.ANY)
    res = pl.pallas_call(
        body, grid=(2, GD_H // GD_HB, nc), in_specs=[tok, tok, tok, col, col, st_spec, tok_d] + [any_spec] * nr,
        out_specs=[tok_d, tok_d, tok_d, col, col] + [any_spec] * nr,
        out_shape=[tok_out, tok_out, tok_out, col_out, col_out] + _exchange_out_shapes(ride, [False] * nr),
        scratch_shapes=[pltpu.VMEM((GD_HB * GD_N, GD_N), F32)] + (_exchange_sems(nr) if nr else []),
        compiler_params=_cparams(("arbitrary", "arbitrary", "arbitrary")), name="gdn_bwd")(
            q, k, v, g, beta, st, do, *ride)
    return tuple(res[:5]), tuple(res[5:])


def _gdn_op(n_ctx):
    @jax.custom_vjp
    def op(q, k, v, g, beta, shards, sinks):
        o, _, gathered = _gdn_forward(q, k, v, g, beta, n_ctx, shards)
        return o, gathered, sinks

    def fwd(q, k, v, g, beta, shards, sinks):
        o, st, gathered = _gdn_forward(q, k, v, g, beta, n_ctx, shards)
        return (o, gathered, sinks), (q, k, v, g, beta, st, shards)

    def bwd(res, cts):
        q, k, v, g, beta, st, shards = res
        do, _, dsinks = cts
        (dq, dk, dv, dg, db), received = _gdn_backward(q, k, v, g, beta, st, do, n_ctx, tuple(dsinks))
        return dq[0] + dq[1], dk[0] + dk[1], dv[0] + dv[1], dg, db, tuple(None for _ in shards), received

    op.defvjp(fwd, bwd)
    return op


def _bdot(a, b):
    return jnp.dot(a.astype(BF16), b.astype(BF16), preferred_element_type=F32)


def _rms(x, eps=NORM_EPS):
    return x * lax.rsqrt(jnp.mean(x * x, axis=-1, keepdims=True) + eps)


def _softplus(x):
    return jnp.maximum(x, 0.0) + jnp.log(1.0 + jnp.exp(-jnp.abs(x)))


def _heads(width, seg):
    e = (lax.div(lax.broadcasted_iota(jnp.int32, (width, 128), 0), seg)
         == lax.broadcasted_iota(jnp.int32, (width, 128), 1)).astype(F32)
    et = (lax.div(lax.broadcasted_iota(jnp.int32, (128, width), 1), seg)
          == lax.broadcasted_iota(jnp.int32, (128, width), 0)).astype(F32)
    return e, et


def _head_sum(x, seg):
    e, et = _heads(x.shape[1], seg)
    return _dot(_dot(x, e), et)


def _l2n(x, seg):
    return x * lax.rsqrt(jnp.maximum(_head_sum(x * x, seg), 1e-12))


def _f_norm_mod(x, g, sc, sh):
    return ((_rms(x) * g) * (1.0 + sc) + sh,)


def _f_rw_prep(pm, k_k, k_a, w0, wup, a0, aup, gup):
    k = pm[:, RW_W:2 * RW_W]
    kk = _l2n(k * k_k, RW_N)
    ws, kds, kas = [], [], []
    for d in range(2):
        wd = pm[:, 3 * RW_W + 128 * d:3 * RW_W + 128 * (d + 1)]
        ad = pm[:, 3 * RW_W + 256 + 128 * d:3 * RW_W + 256 + 128 * (d + 1)]
        wlog = -_softplus(-(w0[d:d + 1] + _bdot(jnp.tanh(wd), wup[128 * d:128 * (d + 1)]))) - 0.5
        ws.append(jnp.exp(-jnp.exp(wlog)))
        a = jax.nn.sigmoid(a0[d:d + 1] + _bdot(ad, aup[128 * d:128 * (d + 1)]))
        kds.append(k * (1.0 + (a - 1.0) * k_a))
        kas.append(kk * a)
    g = _bdot(jax.nn.sigmoid(pm[:, 3 * RW_W + 512:3 * RW_W + 512 + GATE_RANK]), gup)
    return kk, ws[0], ws[1], kds[0], kds[1], kas[0], kas[1], g


def _f_rw_read(o0, o1, r, kd0, kd1, v, g, r_k, gn_g, gn_b):
    o = o0 + o1
    c = o - _head_sum(o, RW_N) * (1.0 / RW_N)
    var = _head_sum(c * c, RW_N) * (1.0 / RW_N)
    on = c * lax.rsqrt(var + RW_GN_EPS) * gn_g + gn_b
    bonus = _head_sum(r * (kd0 + kd1) * r_k, RW_N) * v
    return ((on + bonus) * g,)


def _f_gd_prep(cq, ab, alog, dtb):
    qkv = cq * jax.nn.sigmoid(cq)
    q = _l2n(qkv[:, :GD_W], GD_N) * (GD_N ** -0.5)
    k = _l2n(qkv[:, GD_W:2 * GD_W], GD_N)
    v = qkv[:, 2 * GD_W:]
    glog = -jnp.exp(alog) * _softplus(ab + dtb)
    lane = lax.broadcasted_iota(jnp.int32, ab.shape, 1)
    return q, k, v, jnp.where(lane < 2 * GD_H, glog, jax.nn.sigmoid(ab))


def _f_gd_read(o0, o1, z, ng):
    o = o0 + o1
    y = o * lax.rsqrt(_head_sum(o * o, GD_N) * (1.0 / GD_N) + NORM_EPS) * jnp.concatenate([ng] * GD_H, axis=1)
    return (y * (z * jax.nn.sigmoid(z)),)


def _f_merge(za, zb, gates):
    return (jax.nn.sigmoid(gates[:, :D]) * za + jax.nn.sigmoid(gates[:, D:]) * zb,)


def _f_res_norm(x, att, g1, ng, sc, sh):
    x1 = x + g1 * att
    return x1, (_rms(x1) * ng) * (1.0 + sc) + sh


def _f_glu(gc, val):
    return (0.5 * gc * (1.0 + lax.erf(gc * (2.0 ** -0.5))) * val,)


def _f_final(x1, ff, target, g2, fg):
    y = _rms(x1 + g2 * ff) * fg
    err = y - target
    return (jnp.broadcast_to(0.5 * jnp.mean(err * err, axis=-1, keepdims=True), (x1.shape[0], 128)),)


def _exchange_copies(ins, outs, bcast, send_sems, recv_sems, local_sems):
    n = len(ins)
    x, y, c = lax.axis_index("x"), lax.axis_index("y"), lax.axis_index("c")
    me = 4 * x + 2 * y + c
    own = [pltpu.make_async_copy(ins[i] if bcast[i] else ins[i].at[me], outs[i].at[me], local_sems.at[i])
           for i in range(n)]
    sends, landings = [], []
    for k in range(1, NDEV):
        kx, ky, kc = (k >> 2) & 1, (k >> 1) & 1, k & 1
        px, py, pc = (1 - x if kx else x), (1 - y if ky else y), (1 - c if kc else c)
        peer = 4 * px + 2 * py + pc
        for i in range(n):
            sem = i * (NDEV - 1) + k - 1

            def copy(dst_block, i=i, sem=sem, peer=peer, dev=(px, py, pc)):
                return pltpu.make_async_remote_copy(
                    src_ref=ins[i] if bcast[i] else ins[i].at[peer], dst_ref=outs[i].at[dst_block],
                    send_sem=send_sems.at[sem], recv_sem=recv_sems.at[sem],
                    device_id=dev, device_id_type=pl.DeviceIdType.MESH)

            sends.append(copy(me))
            landings.append(copy(peer))

    def start():
        for cp in own + sends:
            cp.start()

    def wait():
        for cp, landing in zip(sends, landings):
            cp.wait_send()
            landing.wait_recv()
        for cp in own:
            cp.wait()

    return start, wait


def _exchange_sems(n):
    return [pltpu.SemaphoreType.DMA((n * (NDEV - 1),)), pltpu.SemaphoreType.DMA((n * (NDEV - 1),)),
            pltpu.SemaphoreType.DMA((n,))]


def _exchange_out_shapes(arrays, bcast):
    return [jax.ShapeDtypeStruct((NDEV,) + (a.shape if b else a.shape[1:]), a.dtype) for a, b in zip(arrays, bcast)]


def _exchange(arrays, bcast, name):
    n = len(arrays)

    def body(*refs):
        start, wait = _exchange_copies(refs[:n], refs[n:2 * n], bcast, *refs[2 * n:])
        start()
        wait()

    any_spec = pl.BlockSpec(memory_space=pl.ANY)
    return pl.pallas_call(
        body, in_specs=[any_spec] * n, out_specs=[any_spec] * n, out_shape=_exchange_out_shapes(arrays, bcast),
        scratch_shapes=_exchange_sems(n),
        compiler_params=pltpu.CompilerParams(has_side_effects=True), name=name)(*arrays)


def _adamw(slabs, w, m, v, name):
    R, C = w.shape
    ns = slabs.shape[0]
    tr = _pick(R, (256, 128, 64, 32, 16, 8))

    def body(s_ref, w_ref, m_ref, v_ref, g_ref, d_ref, nm_ref, nv_ref):
        g = s_ref[0].astype(F32)
        for i in range(1, ns):
            g = g + s_ref[i].astype(F32)
        nm = ADAM_B1 * m_ref[...] + (1.0 - ADAM_B1) * g
        nv = ADAM_B2 * v_ref[...] + (1.0 - ADAM_B2) * (g * g)
        m_hat = nm / (1.0 - ADAM_B1 ** ADAM_STEP)
        v_hat = nv / (1.0 - ADAM_B2 ** ADAM_STEP)
        g_ref[...] = g
        d_ref[...] = -ADAM_LR * (m_hat / (jnp.sqrt(v_hat) + ADAM_EPS) + ADAM_WD * w_ref[...])
        nm_ref[...] = nm
        nv_ref[...] = nv

    blk = pl.BlockSpec((tr, C), lambda i: (i, 0))
    out = jax.ShapeDtypeStruct((R, C), F32)
    return pl.pallas_call(
        body, grid=(R // tr,), in_specs=[pl.BlockSpec((ns, tr, C), lambda i: (0, i, 0)), blk, blk, blk],
        out_specs=[blk] * 4, out_shape=[out] * 4, compiler_params=_cparams(("parallel",)), name=name)(slabs, w, m, v)


def _silu(z):
    return z * jax.nn.sigmoid(z)


def _ada_forward(c_ext, w, b):
    n = w.shape[1]
    tn = _pick(n, (512, 256, 128))

    def body(c_ref, w_ref, b_ref, o_ref):
        o_ref[...] = _bdot(_silu(c_ref[...]), w_ref[...]) + b_ref[...]

    return pl.pallas_call(
        body, grid=(n // tn,),
        in_specs=[pl.BlockSpec(c_ext.shape, lambda j: (0, 0)), pl.BlockSpec((D, tn), lambda j: (0, j)),
                  pl.BlockSpec((1, tn), lambda j: (0, j))],
        out_specs=pl.BlockSpec((16, tn), lambda j: (0, j)), out_shape=jax.ShapeDtypeStruct((16, n), F32),
        compiler_params=_cparams(("parallel",)), name="ada_fwd")(c_ext, w, b)


def _ada_dmod(lat, ctxr):
    n = lat.shape[1]
    tn = 2048

    def body(lat_ref, ctx_ref, o_ref):
        o_ref[...] = jnp.concatenate([lat_ref[...], jnp.broadcast_to(jnp.sum(ctx_ref[...], axis=0, keepdims=True), (8, tn))], axis=0)

    blk = pl.BlockSpec((NDEV, tn), lambda j: (0, j))
    return pl.pallas_call(
        body, grid=(n // tn,), in_specs=[blk, blk],
        out_specs=pl.BlockSpec((16, tn), lambda j: (0, j)), out_shape=jax.ShapeDtypeStruct((16, n), F32),
        compiler_params=_cparams(("parallel",)), name="ada_dmod")(lat, ctxr)


def _f_colsum16(dm):
    return (jnp.broadcast_to(jnp.sum(dm, axis=0, keepdims=True), dm.shape),)


def _f_silu_grad(dpart, c_ext):
    sg = jax.nn.sigmoid(c_ext)
    return (dpart * sg * (1.0 + c_ext * (1.0 - sg)),)


def _f_silu(c_ext):
    return (_silu(c_ext),)


def _total_forward(rows):
    n = rows.shape[0]
    tm = _pick(n, (256, 128, 64, 32, 16, 8))

    def body(r_ref, o_ref):
        @pl.when(pl.program_id(0) == 0)
        def _():
            o_ref[...] = jnp.zeros_like(o_ref)

        o_ref[...] += jnp.broadcast_to(jnp.sum(r_ref[...], axis=0, keepdims=True), (8, 128))

    return pl.pallas_call(
        body, grid=(n // tm,), in_specs=[pl.BlockSpec((tm, 128), lambda i: (i, 0))],
        out_specs=pl.BlockSpec((8, 128), lambda i: (0, 0)), out_shape=jax.ShapeDtypeStruct((8, 128), F32),
        compiler_params=_cparams(("arbitrary",)), name="loss_total")(rows)


@jax.custom_vjp
def _total(rows):
    return _total_forward(rows)[0, 0]


def _total_fwd(rows):
    return _total_forward(rows)[0, 0], rows


def _total_bwd(rows, ct):
    lane = lax.broadcasted_iota(jnp.int32, rows.shape, 1)
    return (jnp.where(lane == 0, ct, 0.0).astype(F32),)


_total.defvjp(_total_fwd, _total_bwd)


def _pad_cols(a, width):
    return jnp.pad(a, ((0, 0), (0, width - a.shape[1])))


def _rw_cols(a):
    parts = [a[:, :3 * RW_W]]
    for i in range(4):
        parts.append(_pad_cols(a[:, 3 * RW_W + RANK * i:3 * RW_W + RANK * (i + 1)], 128))
    parts.append(a[:, 3 * RW_W + 4 * RANK:])
    return jnp.concatenate(parts, axis=1)


RW_COLS = 3 * RW_W + 4 * RANK + GATE_RANK
GD_COLS = 4 * GD_W + 4 * GD_H


def _split_w_in(w):
    gd = w[:, RW_COLS:RW_COLS + GD_COLS]
    rw = jnp.concatenate([_rw_cols(w[:, :RW_COLS]), _pad_cols(gd[:, 4 * GD_W:], 256)], axis=1)
    return w[:, RW_COLS + GD_COLS:], rw, gd[:, :4 * GD_W]


def _unshard(g, col):
    if col:
        return jnp.swapaxes(g, 0, 1).reshape(g.shape[1], NDEV * g.shape[2])
    return g.reshape(NDEV * g.shape[1], g.shape[2])


def _reshard(a, col):
    if col:
        return jnp.swapaxes(a.reshape(a.shape[0], NDEV, a.shape[1] // NDEV), 0, 1)
    return a.reshape(NDEV, a.shape[0] // NDEV, a.shape[1])


def _pack(parts, width, row_mult):
    flat = [p.reshape(-1) for p in parts]
    offs, o = [], 0
    for f in flat:
        offs.append(o)
        o += f.shape[0]
    rows = -(-o // (width * row_mult)) * row_mult
    cat = jnp.concatenate(flat + [jnp.zeros((rows * width - o,), flat[0].dtype)])
    return cat.reshape(rows, width), offs


BIG = ("w_in", "w_a_out", "w_b_out", "w_o", "ffn_w1", "ffn_w2")
LATE = BIG[1:]
SMALL = ("rw_w0", "rw_w_up", "rw_a0", "rw_a_up", "rw_g_up", "gd_conv_w", "ffn_conv_w")
ROW_SHARDED = ("w_o", "ffn_w2")
REPL = ("norm1_g", "norm2_g", "rw_mu", "rw_k_k", "rw_k_a", "rw_r_k", "rw_gn_g", "rw_gn_b", "gd_a_log", "gd_dt_bias",
        "gd_norm_g", "final_norm_g")
WEIGHTS = ('c_ctx', 'w_ada', 'b_ada', 'norm1_g', 'norm2_g', 'w_in', 'rw_mu', 'rw_k_k', 'rw_k_a', 'rw_r_k', 'rw_w0', 'rw_w_up', 'rw_a0', 'rw_a_up', 'rw_g_up', 'rw_gn_g', 'rw_gn_b', 'gd_conv_w', 'gd_a_log', 'gd_dt_bias', 'gd_norm_g', 'w_a_out', 'w_b_out', 'w_o', 'ffn_w1', 'ffn_conv_w', 'ffn_w2', 'final_norm_g')


def _view2d(a):
    return a.reshape(-1, a.shape[-1])


def _layer_loss(x, modl, modc, sinks, small, repl, ctx, target, big, n_ctx):
    sh1, sc1, g1, sh2, sc2, g2 = [modl[:, i * D:(i + 1) * D] for i in range(6)]
    csh1, csc1 = modc[:, :D], modc[:, D:2 * D]
    n1g, n2g, fg = repl["norm1_g"], repl["norm2_g"], repl["final_norm_g"].reshape(1, D)
    (h_lat,) = _rowwise(_f_norm_mod, "norm1_lat", tm=256)((x,), (n1g, sc1, sh1))
    (h_ctx,) = _rowwise(_f_norm_mod, "norm1_ctx", tm=256)((ctx,), (n1g, csc1, csh1))
    h = jnp.concatenate([h_ctx, h_lat], axis=0)
    wg, wr, wd = _split_w_in(big["w_in"])
    sg, sr, sd = _split_w_in(_unshard(sinks["w_in"], True))
    p_gate = _dense("proj_gate")(h_lat, wg, sg)
    p_rw = _dense("proj_rw")(h, wr, sr)
    p_gd = _dense("proj_gd")(h, wd, sd)
    n = h.shape[0]
    lat = slice(n_ctx, n)

    mu = _rw_cols(repl["rw_mu"])
    coef = jnp.concatenate([0.5 * mu, 1.0 - mu, 0.5 * mu], axis=0)
    pm = _stencil("rw_shift", ("seg", (n_ctx,)), ncols=RW_COLS + 128, tm=256, tc=768)(p_rw, coef)
    ab = p_rw[:, RW_COLS + 128:RW_COLS + 256]
    pad_rank = lambda a: jnp.pad(a, ((0, 0), (0, 128 - RANK), (0, 0))).reshape(256, RW_W)
    kk, w0, w1, kd0, kd1, ka0, ka1, g = _rowwise(_f_rw_prep, "rw_prep", tm=128)(
        (pm,), (repl["rw_k_k"], repl["rw_k_a"], small["rw_w0"], pad_rank(small["rw_w_up"]), small["rw_a0"],
                pad_rank(small["rw_a_up"]), small["rw_g_up"]))
    r, v = pm[:, :RW_W], pm[:, 2 * RW_W:3 * RW_W]
    o = _wkv_op(n_ctx)(r, v, kk, jnp.stack([w0, w1]), jnp.stack([kd0, kd1]), jnp.stack([ka0, ka1]))
    (ya,) = _rowwise(_f_rw_read, "rw_read", tm=256)(
        (o[0, lat], o[1, lat], r[lat], kd0[lat], kd1[lat], v[lat], g[lat]),
        (repl["rw_r_k"], repl["rw_gn_g"], repl["rw_gn_b"]))

    cq = _stencil("gd_conv", ("seg", (n_ctx,)), ncols=3 * GD_W, tm=256, tc=768)(p_gd, small["gd_conv_w"])
    z = p_gd[lat, 3 * GD_W:]
    lanes16 = lambda a: _pad_cols(a.reshape(1, 2 * GD_H), 128)
    q, k, v2, gb = _rowwise(_f_gd_prep, "gd_prep", tm=256)(
        (cq, ab), (lanes16(repl["gd_a_log"]), lanes16(repl["gd_dt_bias"])))
    per_head = lambda a: a.T.reshape(2, GD_H, n, 1)
    og, gathered, late_sinks = _gdn_op(n_ctx)(
        q, k, v2, per_head(gb[:, :2 * GD_H]), per_head(gb[:, 2 * GD_H:4 * GD_H]),
        tuple(big[k_] for k_ in LATE), tuple(sinks[k_] for k_ in LATE))
    (yb,) = _rowwise(_f_gd_read, "gd_read", tm=256)((og[0, lat], og[1, lat], z), (repl["gd_norm_g"],))
    w = {k_: _unshard(g_, k_ not in ROW_SHARDED) for k_, g_ in zip(LATE, gathered)}
    s = {k_: _unshard(s_, k_ not in ROW_SHARDED) for k_, s_ in zip(LATE, late_sinks)}

    za = _dense("a_out")(ya, w["w_a_out"], s["w_a_out"])
    zb = _dense("b_out")(yb, w["w_b_out"], s["w_b_out"])
    (merged,) = _rowwise(_f_merge, "merge", tm=256)((za, zb, p_gate), ())
    att = _dense("w_o")(merged, w["w_o"], s["w_o"])
    x1, h2 = _rowwise(_f_res_norm, "res_norm2", tm=256)((x, att), (g1, n2g, sc2, sh2))
    ug = _dense("ffn_gate")(h2, w["ffn_w1"][:, :D_FF], s["ffn_w1"][:, :D_FF])
    uv = _dense("ffn_val")(h2, w["ffn_w1"][:, D_FF:], s["ffn_w1"][:, D_FF:])
    gc = _stencil("ffn_conv", ("grid", GRID_W), ncols=D_FF, tm=256, tc=1408)(ug, small["ffn_conv_w"])
    (act,) = _rowwise(_f_glu, "glu", tm=64)((gc, uv), ())
    ff = _dense("ffn_w2")(act, w["ffn_w2"], s["ffn_w2"])
    (rows,) = _rowwise(_f_final, "final", tm=256, n_nondiff=1)((x1, ff, target), (g2, fg))
    return _total(rows)


def kernel(x, c, ctx, c_ctx, w_ada, b_ada, norm1_g, norm2_g, w_in, rw_mu, rw_k_k, rw_k_a, rw_r_k, rw_w0, rw_w_up, rw_a0, rw_a_up, rw_g_up, rw_gn_g, rw_gn_b, gd_conv_w, gd_a_log, gd_dt_bias, gd_norm_g, w_a_out, w_b_out, w_o, ffn_w1, ffn_conv_w, ffn_w2, final_norm_g, loss_target, m_c_ctx, m_w_ada, m_b_ada, m_norm1_g, m_norm2_g, m_w_in, m_rw_mu, m_rw_k_k, m_rw_k_a, m_rw_r_k, m_rw_w0, m_rw_w_up, m_rw_a0, m_rw_a_up, m_rw_g_up, m_rw_gn_g, m_rw_gn_b, m_gd_conv_w, m_gd_a_log, m_gd_dt_bias, m_gd_norm_g, m_w_a_out, m_w_b_out, m_w_o, m_ffn_w1, m_ffn_conv_w, m_ffn_w2, m_final_norm_g, v_c_ctx, v_w_ada, v_b_ada, v_norm1_g, v_norm2_g, v_w_in, v_rw_mu, v_rw_k_k, v_rw_k_a, v_rw_r_k, v_rw_w0, v_rw_w_up, v_rw_a0, v_rw_a_up, v_rw_g_up, v_rw_gn_g, v_rw_gn_b, v_gd_conv_w, v_gd_a_log, v_gd_dt_bias, v_gd_norm_g, v_w_a_out, v_w_b_out, v_w_o, v_ffn_w1, v_ffn_conv_w, v_ffn_w2, v_final_norm_g):
    args = dict(x=x, c=c, ctx=ctx, c_ctx=c_ctx, w_ada=w_ada, b_ada=b_ada, norm1_g=norm1_g, norm2_g=norm2_g, w_in=w_in, rw_mu=rw_mu, rw_k_k=rw_k_k, rw_k_a=rw_k_a, rw_r_k=rw_r_k, rw_w0=rw_w0, rw_w_up=rw_w_up, rw_a0=rw_a0, rw_a_up=rw_a_up, rw_g_up=rw_g_up, rw_gn_g=rw_gn_g, rw_gn_b=rw_gn_b, gd_conv_w=gd_conv_w, gd_a_log=gd_a_log, gd_dt_bias=gd_dt_bias, gd_norm_g=gd_norm_g, w_a_out=w_a_out, w_b_out=w_b_out, w_o=w_o, ffn_w1=ffn_w1, ffn_conv_w=ffn_conv_w, ffn_w2=ffn_w2, final_norm_g=final_norm_g, loss_target=loss_target, m_c_ctx=m_c_ctx, m_w_ada=m_w_ada, m_b_ada=m_b_ada, m_norm1_g=m_norm1_g, m_norm2_g=m_norm2_g, m_w_in=m_w_in, m_rw_mu=m_rw_mu, m_rw_k_k=m_rw_k_k, m_rw_k_a=m_rw_k_a, m_rw_r_k=m_rw_r_k, m_rw_w0=m_rw_w0, m_rw_w_up=m_rw_w_up, m_rw_a0=m_rw_a0, m_rw_a_up=m_rw_a_up, m_rw_g_up=m_rw_g_up, m_rw_gn_g=m_rw_gn_g, m_rw_gn_b=m_rw_gn_b, m_gd_conv_w=m_gd_conv_w, m_gd_a_log=m_gd_a_log, m_gd_dt_bias=m_gd_dt_bias, m_gd_norm_g=m_gd_norm_g, m_w_a_out=m_w_a_out, m_w_b_out=m_w_b_out, m_w_o=m_w_o, m_ffn_w1=m_ffn_w1, m_ffn_conv_w=m_ffn_conv_w, m_ffn_w2=m_ffn_w2, m_final_norm_g=m_final_norm_g, v_c_ctx=v_c_ctx, v_w_ada=v_w_ada, v_b_ada=v_b_ada, v_norm1_g=v_norm1_g, v_norm2_g=v_norm2_g, v_w_in=v_w_in, v_rw_mu=v_rw_mu, v_rw_k_k=v_rw_k_k, v_rw_k_a=v_rw_k_a, v_rw_r_k=v_rw_r_k, v_rw_w0=v_rw_w0, v_rw_w_up=v_rw_w_up, v_rw_a0=v_rw_a0, v_rw_a_up=v_rw_a_up, v_rw_g_up=v_rw_g_up, v_rw_gn_g=v_rw_gn_g, v_rw_gn_b=v_rw_gn_b, v_gd_conv_w=v_gd_conv_w, v_gd_a_log=v_gd_a_log, v_gd_dt_bias=v_gd_dt_bias, v_gd_norm_g=v_gd_norm_g, v_w_a_out=v_w_a_out, v_w_b_out=v_w_b_out, v_w_o=v_w_o, v_ffn_w1=v_ffn_w1, v_ffn_conv_w=v_ffn_conv_w, v_ffn_w2=v_ffn_w2, v_final_norm_g=v_final_norm_g)
    me = 4 * lax.axis_index("x") + 2 * lax.axis_index("y") + lax.axis_index("c")
    x2, ctx2, target = args["x"][0], args["ctx"][0], args["loss_target"][0]
    shard = {k: _view2d(args[k][0]) for k in BIG + SMALL}

    small_pack, small_offs = _pack([args["c"]] + [shard[k] for k in SMALL], 128, 8)
    gathered = _exchange([shard["w_in"].astype(BF16), small_pack], [True, True], "gather_weights")
    small_all = gathered[-1].reshape(NDEV, -1)

    def unpack(allg, off, k):
        r_, c_ = shard[k].shape
        return _unshard(allg[:, off:off + r_ * c_].reshape(NDEV, r_, c_), k not in ROW_SHARDED)

    big = {k: shard[k].astype(BF16) for k in LATE}
    big["w_in"] = _unshard(gathered[0], True)
    small = {k: unpack(small_all, o, k) for k, o in zip(SMALL, small_offs[1:])}
    small["rw_w_up"] = small["rw_w_up"].reshape(2, RANK, RW_W)
    small["rw_a_up"] = small["rw_a_up"].reshape(2, RANK, RW_W)
    c_all = small_all[:, :D]

    c_ext = jnp.concatenate([c_all, args["c_ctx"].reshape(1, D), jnp.zeros((7, D), F32)], axis=0)
    w_ada = args["w_ada"][0]
    nb = w_ada.shape[1]
    mod_blk = _ada_forward(c_ext, w_ada, lax.dynamic_slice(args["b_ada"], (0, me * nb), (1, nb)))
    (mod_all,) = _exchange([mod_blk], [True], "gather_mod")
    mod_all = jnp.swapaxes(mod_all, 0, 1).reshape(16, NDEV * nb)
    modl = lax.dynamic_slice(mod_all, (me, 0), (1, NDEV * nb))
    modc = mod_all[NDEV:NDEV + 1]

    repl = {k: args[k] for k in REPL}
    sinks = {k: jnp.zeros((NDEV,) + shard[k].shape, GRAD_WIRE) for k in BIG}
    n_ctx = ctx2.shape[0]
    loss, grads = jax.value_and_grad(_layer_loss, argnums=(0, 1, 2, 3, 4, 5))(
        x2, modl, modc, sinks, small, repl, ctx2, target, big, n_ctx)
    dx, dmodl, dmodc, dbig, dsmall, drepl = grads
    loss = lax.psum(loss, ("x", "y", "c"))

    rep_pack, rep_offs = _pack([dmodl, dmodc] + [drepl[k] for k in REPL], 128, 8)
    (rep_all,) = _exchange([rep_pack], [True], "gather_small_grads")
    rep_all = rep_all.reshape(NDEV, -1)
    six_d = NDEV * nb
    dmod_ext = _ada_dmod(rep_all[:, :six_d], rep_all[:, six_d:2 * six_d])
    dmod_blk = lax.dynamic_slice(dmod_ext, (0, me * nb), (16, nb))
    (s_ext,) = _rowwise(_f_silu, "ada_silu", tm=16)((c_ext,), ())
    g_w_ada = _mm(s_ext, dmod_blk, ta=True, name="ada_dw")
    dpart = _mm(dmod_blk, w_ada, tb=True, name="ada_dc")
    (dpart,) = _rowwise(_f_silu_grad, "ada_dsilu", tm=16)((dpart, c_ext), ())

    flat_small = [_reshard(_view2d(dsmall[k]) if k != "ffn_conv_w" else dsmall[k], True).reshape(NDEV, -1) for k in SMALL]
    small_send = jnp.concatenate(flat_small, axis=1)
    pad = -small_send.shape[1] % 1024
    small_send = jnp.pad(small_send, ((0, 0), (0, pad))).reshape(NDEV, -1, 128)
    got = _exchange([dbig["w_in"], small_send, dpart[NDEV:NDEV + 1]], [False, False, True], "scatter_grads")
    small_got = got[1].reshape(NDEV, -1)

    slabs = {k: dbig[k] for k in LATE}
    slabs["w_in"] = got[0]
    o = 0
    for k in SMALL:
        r_, c_ = shard[k].shape
        slabs[k] = small_got[:, o:o + r_ * c_].reshape(NDEV, r_, c_)
        o += r_ * c_
    slabs["w_ada"] = g_w_ada[None]
    slabs["c_ctx"] = got[2]
    slabs["b_ada"] = jnp.concatenate([rep_all[:, :six_d], rep_all[:, six_d:2 * six_d]], axis=0)[:, None, :]
    for k, off in zip(REPL, rep_offs[2:]):
        size = args[k].size
        slabs[k] = rep_all[:, off:off + size].reshape((NDEV,) + _view2d(args[k]).shape)

    outs = {}
    for k in WEIGHTS:
        shape = args[k].shape
        res = _adamw(slabs[k], _view2d(args[k]), _view2d(args["m_" + k]), _view2d(args["v_" + k]), "adamw_" + k)
        outs[k] = [a.reshape(shape) for a in res]
    return (loss, dx[None]) + tuple(outs[k][i] for i in range(4) for k in WEIGHTS)
```

```python
import functools
import math

import jax
import jax.numpy as jnp
from jax import lax
from jax.experimental import pallas as pl
from jax.experimental.pallas import tpu as pltpu

F32 = jnp.float32
BF16 = jnp.bfloat16
HI = lax.Precision.HIGHEST
GRAD_WIRE = BF16

D = 2048
NCTX = 256
GRID_W = 64
NORM_EPS = 1e-6
RW_H, RW_N = 16, 64
RW_W = RW_H * RW_N
RANK = 96
GATE_RANK = 256
RW_GN_EPS = 64e-5
GD_H, GD_N = 8, 128
GD_W = GD_H * GD_N
CHUNK = 64
D_FF = 5632
NDEV = 8
PW = 4096
VMEM_LIMIT = 56 * 1024 * 1024
MM_VMEM_BUDGET = 46 * 1024 * 1024

ADAM_LR, ADAM_B1, ADAM_B2, ADAM_EPS, ADAM_WD, ADAM_STEP = 0.001, 0.9, 0.999, 1e-08, 0.01, 10


def _cparams(sem):
    return pltpu.CompilerParams(dimension_semantics=sem, vmem_limit_bytes=VMEM_LIMIT)


def _pick(n, cands):
    for c in cands:
        if n % c == 0:
            return c
    return n


def _mm(a, b, *, ta=False, tb=False, name, out_dtype=F32):
    M = a.shape[1] if ta else a.shape[0]
    K = a.shape[0] if ta else a.shape[1]
    N = b.shape[0] if tb else b.shape[1]
    assert K == (b.shape[1] if tb else b.shape[0])
    tm = _pick(M, (512, 768, 256, 128, 64, 16, 8))

    def vmem_bytes(tn_):
        return (2 * tm * K * a.dtype.itemsize + tm * K * 2 + 2 * tn_ * K * b.dtype.itemsize
                + 2 * tm * tn_ * jnp.dtype(out_dtype).itemsize)

    tn = next(t for t in (1024, 512, 256, 128) if N % t == 0 and (vmem_bytes(t) <= MM_VMEM_BUDGET or t == 128))
    dn = (((0 if ta else 1,), (1 if tb else 0,)), ((), ()))

    def body(a_ref, b_ref, o_ref, a16):
        @pl.when(pl.program_id(1) == 0)
        def _():
            a16[...] = a_ref[...].astype(BF16)

        o_ref[...] = lax.dot_general(a16[...], b_ref[...].astype(BF16), dn,
                                     preferred_element_type=F32).astype(out_dtype)

    a_blk = (K, tm) if ta else (tm, K)
    a_spec = pl.BlockSpec(a_blk, (lambda i, j: (0, i)) if ta else (lambda i, j: (i, 0)))
    b_spec = pl.BlockSpec((tn, K), lambda i, j: (j, 0)) if tb else pl.BlockSpec((K, tn), lambda i, j: (0, j))
    return pl.pallas_call(
        body, grid=(M // tm, N // tn), in_specs=[a_spec, b_spec],
        out_specs=pl.BlockSpec((tm, tn), lambda i, j: (i, j)),
        out_shape=jax.ShapeDtypeStruct((M, N), out_dtype),
        scratch_shapes=[pltpu.VMEM(a_blk, BF16)],
        compiler_params=_cparams(("parallel", "arbitrary")), name=name)(a, b)


def _dense(name):
    @jax.custom_vjp
    def f(a, w, sink):
        return _mm(a, w, name=name + "_fwd")

    def fwd(a, w, sink):
        return _mm(a, w, name=name + "_fwd"), (a, w)

    def bwd(res, dc):
        a, w = res
        return (_mm(dc, w, tb=True, name=name + "_da"), None,
                _mm(a, dc, ta=True, name=name + "_dw", out_dtype=GRAD_WIRE))

    f.defvjp(fwd, bwd)
    return f


def _rowwise(fn, name, *, tm, n_nondiff=0):
    def shapes(rows, params):
        tiles = [jax.ShapeDtypeStruct((tm, r.shape[1]), r.dtype) for r in rows]
        ps = [jax.ShapeDtypeStruct(p.shape, p.dtype) for p in params]
        return tiles, ps, jax.eval_shape(fn, *tiles, *ps)

    def row_spec(width):
        return pl.BlockSpec((tm, width), lambda i: (i, 0))

    def whole_spec(shape):
        return pl.BlockSpec(shape, lambda i: (0,) * len(shape))

    def forward(rows, params):
        n = rows[0].shape[0]
        tiles, ps, outs = shapes(rows, params)
        nin = len(rows) + len(params)

        def body(*refs):
            res = fn(*[r[...] for r in refs[:nin]])
            for o_ref, v in zip(refs[nin:], res):
                o_ref[...] = v.astype(o_ref.dtype)

        return pl.pallas_call(
            body, grid=(n // tm,),
            in_specs=[row_spec(t.shape[1]) for t in tiles] + [whole_spec(p.shape) for p in ps],
            out_specs=[row_spec(o.shape[1]) for o in outs],
            out_shape=[jax.ShapeDtypeStruct((n, o.shape[1]), o.dtype) for o in outs],
            compiler_params=_cparams(("parallel",)), name=name + "_fwd")(*rows, *params)

    def backward(rows, params, cts):
        n = rows[0].shape[0]
        tiles, ps, outs = shapes(rows, params)
        nr, npar, nout = len(rows), len(params), len(outs)
        nd = nr - n_nondiff

        def body(*refs):
            ins = [r[...] for r in refs[:nr + npar]]
            ct = tuple(r[...] for r in refs[nr + npar:nr + npar + nout])
            out_refs = refs[nr + npar + nout:]
            fixed = ins[nd:nr]

            def g(*diff):
                return fn(*diff[:nd], *fixed, *diff[nd:])

            _, vjp = jax.vjp(g, *ins[:nd], *ins[nr:])
            grads = vjp(ct)
            for o_ref, v in zip(out_refs[:nd], grads[:nd]):
                o_ref[...] = v.astype(o_ref.dtype)

            @pl.when(pl.program_id(0) == 0)
            def _():
                for o_ref in out_refs[nd:]:
                    o_ref[...] = jnp.zeros_like(o_ref)

            for o_ref, v in zip(out_refs[nd:], grads[nd:]):
                o_ref[...] += v

        res = pl.pallas_call(
            body, grid=(n // tm,),
            in_specs=[row_spec(t.shape[1]) for t in tiles] + [whole_spec(p.shape) for p in ps]
            + [row_spec(o.shape[1]) for o in outs],
            out_specs=[row_spec(t.shape[1]) for t in tiles[:nd]] + [whole_spec(p.shape) for p in ps],
            out_shape=[jax.ShapeDtypeStruct((n, t.shape[1]), t.dtype) for t in tiles[:nd]]
            + [jax.ShapeDtypeStruct(p.shape, p.dtype) for p in ps],
            compiler_params=_cparams(("arbitrary",)), name=name + "_bwd")(*rows, *params, *cts)
        return tuple(res[:nd]) + (None,) * n_nondiff, tuple(res[nd:])

    @jax.custom_vjp
    def op(rows, params):
        return tuple(forward(rows, params))

    def op_fwd(rows, params):
        return tuple(forward(rows, params)), (rows, params)

    def op_bwd(res, cts):
        return backward(res[0], res[1], cts)

    op.defvjp(op_fwd, op_bwd)
    return op


def _valid(t, off, n, mode):
    s = t + off[0]
    ok = (s >= 0) & (s < n)
    if mode[0] == "seg":
        for b in mode[1]:
            ok = ok & ((t >= b) == (s >= b))
    else:
        col = lax.rem(t, mode[1]) + off[1]
        ok = ok & (col >= 0) & (col < mode[1])
    return ok


def _stencil_offsets(mode):
    if mode[0] == "seg":
        return [(-1, 0), (0, 0), (1, 0)]
    w = mode[1]
    return [(di * w + dj, dj) for di in (-1, 0, 1) for dj in (-1, 0, 1)]


def _stencil(name, mode, *, ncols, tm, tc):
    offs = _stencil_offsets(mode)
    J = len(offs)
    halo = 8 if mode[0] == "seg" else 128

    def x_specs(n):
        nb = n // halo
        r = tm // halo
        return [pl.BlockSpec((halo, tc), lambda i, j: (jnp.maximum(i * r - 1, 0), j)),
                pl.BlockSpec((tm, tc), lambda i, j: (i, j)),
                pl.BlockSpec((halo, tc), lambda i, j: (jnp.minimum((i + 1) * r, nb - 1), j))]

    def fill(buf, prev, cur, nxt):
        buf[0:halo, :] = prev[...]
        buf[halo:halo + tm, :] = cur[...]
        buf[halo + tm:, :] = nxt[...]

    def forward(x, coef, suffix=""):
        n = x.shape[0]

        def body(prev, cur, nxt, c_ref, o_ref, buf):
            fill(buf, prev, cur, nxt)
            t = pl.program_id(0) * tm + lax.broadcasted_iota(jnp.int32, (tm, 1), 0)
            acc = jnp.zeros((tm, tc), F32)
            for j, off in enumerate(offs):
                xs = buf[halo + off[0]:halo + off[0] + tm, :]
                acc = acc + jnp.where(_valid(t, off, n, mode), xs, 0.0) * c_ref[j:j + 1, :]
            o_ref[...] = acc

        return pl.pallas_call(
            body, grid=(n // tm, ncols // tc),
            in_specs=x_specs(n) + [pl.BlockSpec((J, tc), lambda i, j: (0, j))],
            out_specs=pl.BlockSpec((tm, tc), lambda i, j: (i, j)),
            out_shape=jax.ShapeDtypeStruct((n, ncols), F32),
            scratch_shapes=[pltpu.VMEM((tm + 2 * halo, tc), F32)],
            compiler_params=_cparams(("parallel", "parallel")), name=name + suffix)(x, x, x, coef)

    def wgrad(x, dy):
        n = x.shape[0]

        def body(prev, cur, nxt, dy_ref, o_ref, buf):
            fill(buf, prev, cur, nxt)
            i = pl.program_id(1)
            t = i * tm + lax.broadcasted_iota(jnp.int32, (tm, 1), 0)

            @pl.when(i == 0)
            def _():
                o_ref[...] = jnp.zeros_like(o_ref)

            dy = dy_ref[...]
            for j, off in enumerate(offs):
                xs = buf[halo + off[0]:halo + off[0] + tm, :]
                o_ref[j:j + 1, :] += jnp.sum(jnp.where(_valid(t, off, n, mode), xs, 0.0) * dy, axis=0, keepdims=True)

        specs = [pl.BlockSpec(s.block_shape, (lambda f: lambda j, i: f(i, j))(s.index_map)) for s in x_specs(n)]
        return pl.pallas_call(
            body, grid=(ncols // tc, n // tm),
            in_specs=specs + [pl.BlockSpec((tm, tc), lambda j, i: (i, j))],
            out_specs=pl.BlockSpec((J, tc), lambda j, i: (0, j)),
            out_shape=jax.ShapeDtypeStruct((J, ncols), F32),
            scratch_shapes=[pltpu.VMEM((tm + 2 * halo, tc), F32)],
            compiler_params=_cparams(("parallel", "arbitrary")), name=name + "_wgrad")(x, x, x, dy)

    @jax.custom_vjp
    def op(x, coef):
        return forward(x, coef)

    def op_fwd(x, coef):
        return forward(x, coef), (x, coef)

    def op_bwd(res, dy):
        x, coef = res
        dx = forward(dy, coef[::-1], "_adj")
        if x.shape[1] != ncols:
            dx = jnp.pad(dx, ((0, 0), (0, x.shape[1] - ncols)))
        return dx, wgrad(x, dy)

    op.defvjp(op_fwd, op_bwd)
    return op


WKV_TC = 32
WKV_TB = 64


def _chunk_index(d, c, n_ctx_chunks, n_chunks):
    rev = jnp.where(c < n_ctx_chunks, n_ctx_chunks - 1 - c, n_ctx_chunks + n_chunks - 1 - c)
    return jnp.where(d == 0, c, rev)


def _to_feature_major(a):
    n = a.shape[0]
    t = a.reshape(n // WKV_TB, WKV_TB, RW_H // 2, 2, RW_N).transpose(2, 4, 0, 3, 1)
    return t.reshape(RW_W // 2, n // WKV_TB * 128)


def _from_feature_major(t):
    n = t.shape[1] // 128 * WKV_TB
    a = t.reshape(RW_H // 2, RW_N, n // WKV_TB, 2, WKV_TB).transpose(2, 4, 0, 3, 1)
    return a.reshape(n, RW_W)


def _pair_consts():
    lane = lax.broadcasted_iota(jnp.int32, (1, 128), 1)
    first = lax.broadcasted_iota(jnp.int32, (RW_N, 128), 1) < RW_N
    same = (lax.div(lax.broadcasted_iota(jnp.int32, (128, 128), 0), RW_N)
            == lax.div(lax.broadcasted_iota(jnp.int32, (128, 128), 1), RW_N))
    return lane < RW_N, lane >= RW_N, first, same.astype(BF16)


def _seg_reduce(x, row, m0, m1, first):
    s0 = jnp.sum(x * jnp.where(m0, row, 0.0), axis=1, keepdims=True)
    s1 = jnp.sum(x * jnp.where(m1, row, 0.0), axis=1, keepdims=True)
    return jnp.where(first, s0, s1)


def _seg_sum_mxu(x, same, passes):
    acc, rest = None, x
    for i in range(passes):
        piece = rest.astype(BF16)
        part = jnp.dot(piece, same, preferred_element_type=F32)
        acc = part if acc is None else acc + part
        if i + 1 < passes:
            rest = rest - piece.astype(F32)
    return acc


def _wkv_forward(r, vT, kk, w, kd, ka, n_ctx):
    n = r.shape[0]
    nc, ncc = n // WKV_TC, n_ctx // WKV_TC
    per_blk = WKV_TB // WKV_TC

    def body(r_ref, vT_ref, kk_ref, w_ref, kd_ref, ka_ref, oT_ref, st_ref, S_ref):
        d, c = pl.program_id(0), pl.program_id(1)
        ci = _chunk_index(d, c, ncc, nc)

        @pl.when(c == 0)
        def _():
            S_ref[...] = jnp.zeros_like(S_ref)

        @pl.when(lax.rem(c, per_blk) == 0)
        def _():
            oT_ref[...] = jnp.zeros_like(oT_ref)

        m0, m1, first, same = _pair_consts()
        lane_t = lax.rem(lax.broadcasted_iota(jnp.int32, (RW_N, 128), 1), WKV_TB)

        def step(s, carry):
            tl = jnp.where(d == 0, s, WKV_TC - 1 - s)
            sel = lane_t == lax.rem(ci, per_blk) * WKV_TC + tl
            row = pl.ds(tl, 1)
            kn_all, w_all, kd_all, ka_all, r_all = -kk_ref[row, :], w_ref[row, :], kd_ref[row, :], ka_ref[row, :], r_ref[row, :]
            pairs = range(RW_H // 2)
            cs = [slice(p * 128, (p + 1) * 128) for p in pairs]
            rows = [slice(p * RW_N, (p + 1) * RW_N) for p in pairs]
            S = [S_ref[p] for p in pairs]
            for p in pairs:
                st_ref[tl, p] = S[p]
            sab = [_seg_reduce(S[p], kn_all[:, cs[p]], m0, m1, first) for p in pairs]
            vb = [_seg_sum_mxu(jnp.where(sel, vT_ref[rows[p], :], 0.0), same, 2) for p in pairs]
            S = [S[p] * w_all[:, cs[p]] + sab[p] * ka_all[:, cs[p]] + vb[p] * kd_all[:, cs[p]] for p in pairs]
            for p in pairs:
                S_ref[p] = S[p]
            for p in pairs:
                ob = _seg_reduce(S[p], r_all[:, cs[p]], m0, m1, first)
                oT_ref[rows[p], :] = jnp.where(sel, ob, oT_ref[rows[p], :])
            return carry

        lax.fori_loop(0, WKV_TC, step, 0, unroll=2)

    def row(dirn):
        if dirn:
            return pl.BlockSpec((None, WKV_TC, RW_W), lambda d, c: (d, _chunk_index(d, c, ncc, nc), 0))
        return pl.BlockSpec((WKV_TC, RW_W), lambda d, c: (_chunk_index(d, c, ncc, nc), 0))

    colT = pl.BlockSpec((RW_W // 2, 128), lambda d, c: (0, lax.div(_chunk_index(d, c, ncc, nc), per_blk)))
    return pl.pallas_call(
        body, grid=(2, nc),
        in_specs=[row(0), colT, row(0), row(1), row(1), row(1)],
        out_specs=[pl.BlockSpec((None, RW_W // 2, 128), lambda d, c: (d, 0, lax.div(_chunk_index(d, c, ncc, nc), per_blk))),
                   pl.BlockSpec((None, WKV_TC, RW_H // 2, RW_N, 128),
                                lambda d, c: (d, _chunk_index(d, c, ncc, nc), 0, 0, 0))],
        out_shape=[jax.ShapeDtypeStruct((2, RW_W // 2, n // WKV_TB * 128), F32),
                   jax.ShapeDtypeStruct((2, n, RW_H // 2, RW_N, 128), F32)],
        scratch_shapes=[pltpu.VMEM((RW_H // 2, RW_N, 128), F32)],
        compiler_params=_cparams(("arbitrary", "arbitrary")), name="wkv_fwd")(r, vT, kk, w, kd, ka)


def _wkv_backward(r, vT, kk, w, kd, ka, st, doT, n_ctx):
    n = r.shape[0]
    nc, ncc = n // WKV_TC, n_ctx // WKV_TC
    per_blk = WKV_TB // WKV_TC

    def body(r_ref, vT_ref, kk_ref, w_ref, kd_ref, ka_ref, st_ref, doT_ref,
             dr_ref, dw_ref, dkd_ref, dkn_ref, dka_ref, dvT_ref, dS_ref):
        d, c = pl.program_id(0), pl.program_id(1)
        ci = _chunk_index(d, nc - 1 - c, ncc, nc)

        @pl.when(c == 0)
        def _():
            dS_ref[...] = jnp.zeros_like(dS_ref)

        @pl.when(lax.rem(c, per_blk) == 0)
        def _():
            dvT_ref[...] = jnp.zeros_like(dvT_ref)

        m0, m1, first, same = _pair_consts()
        lane_t = lax.rem(lax.broadcasted_iota(jnp.int32, (RW_N, 128), 1), WKV_TB)

        def step(s, carry):
            tl = jnp.where(d == 0, WKV_TC - 1 - s, s)
            sel = lane_t == lax.rem(ci, per_blk) * WKV_TC + tl
            row = pl.ds(tl, 1)
            kn_all, w_all, kd_all, ka_all, r_all = -kk_ref[row, :], w_ref[row, :], kd_ref[row, :], ka_ref[row, :], r_ref[row, :]
            pairs = range(RW_H // 2)
            cs = [slice(p * 128, (p + 1) * 128) for p in pairs]
            rows = [slice(p * RW_N, (p + 1) * RW_N) for p in pairs]
            colsum = lambda a: jnp.sum(a, axis=0, keepdims=True)
            Sp = [st_ref[tl, p] for p in pairs]
            dob = [_seg_sum_mxu(jnp.where(sel, doT_ref[rows[p], :], 0.0), same, 2) for p in pairs]
            dS = [dS_ref[p] + dob[p] * r_all[:, cs[p]] for p in pairs]
            dsab = [_seg_reduce(dS[p], ka_all[:, cs[p]], m0, m1, first) for p in pairs]
            sab = [_seg_reduce(Sp[p], kn_all[:, cs[p]], m0, m1, first) for p in pairs]
            vb = [_seg_sum_mxu(jnp.where(sel, vT_ref[rows[p], :], 0.0), same, 2) for p in pairs]
            St = [Sp[p] * w_all[:, cs[p]] + sab[p] * ka_all[:, cs[p]] + vb[p] * kd_all[:, cs[p]] for p in pairs]
            dr_ref[row, :] = jnp.concatenate([colsum(St[p] * dob[p]) for p in pairs], axis=1)
            dw_ref[row, :] = jnp.concatenate([colsum(dS[p] * Sp[p]) for p in pairs], axis=1)
            dka_ref[row, :] = jnp.concatenate([colsum(dS[p] * sab[p]) for p in pairs], axis=1)
            dkd_ref[row, :] = jnp.concatenate([colsum(dS[p] * vb[p]) for p in pairs], axis=1)
            dkn_ref[row, :] = jnp.concatenate([colsum(Sp[p] * dsab[p]) for p in pairs], axis=1)
            for p in pairs:
                dvb = _seg_reduce(dS[p], kd_all[:, cs[p]], m0, m1, first)
                dvT_ref[rows[p], :] = jnp.where(sel, dvb, dvT_ref[rows[p], :])
                dS_ref[p] = dS[p] * w_all[:, cs[p]] + dsab[p] * kn_all[:, cs[p]]
            return carry

        lax.fori_loop(0, WKV_TC, step, 0, unroll=2)

    def cidx(d, c):
        return _chunk_index(d, nc - 1 - c, ncc, nc)

    def row(dirn):
        if dirn:
            return pl.BlockSpec((None, WKV_TC, RW_W), lambda d, c: (d, cidx(d, c), 0))
        return pl.BlockSpec((WKV_TC, RW_W), lambda d, c: (cidx(d, c), 0))

    colT = pl.BlockSpec((RW_W // 2, 128), lambda d, c: (0, lax.div(cidx(d, c), per_blk)))
    colT_d = pl.BlockSpec((None, RW_W // 2, 128), lambda d, c: (d, 0, lax.div(cidx(d, c), per_blk)))
    st_spec = pl.BlockSpec((None, WKV_TC, RW_H // 2, RW_N, 128), lambda d, c: (d, cidx(d, c), 0, 0, 0))
    rows_out = jax.ShapeDtypeStruct((2, n, RW_W), F32)
    return pl.pallas_call(
        body, grid=(2, nc),
        in_specs=[row(0), colT, row(0), row(1), row(1), row(1), st_spec, colT_d],
        out_specs=[row(1)] * 5 + [colT_d],
        out_shape=[rows_out] * 5 + [jax.ShapeDtypeStruct((2, RW_W // 2, n // WKV_TB * 128), F32)],
        scratch_shapes=[pltpu.VMEM((RW_H // 2, RW_N, 128), F32)],
        compiler_params=_cparams(("arbitrary", "arbitrary")), name="wkv_bwd")(r, vT, kk, w, kd, ka, st, doT)


def _wkv_op(n_ctx):
    def readout(oT):
        return jnp.stack([_from_feature_major(oT[0]), _from_feature_major(oT[1])])

    @jax.custom_vjp
    def op(r, v, kk, w, kd, ka):
        return readout(_wkv_forward(r, _to_feature_major(v), kk, w, kd, ka, n_ctx)[0])

    def fwd(r, v, kk, w, kd, ka):
        vT = _to_feature_major(v)
        oT, st = _wkv_forward(r, vT, kk, w, kd, ka, n_ctx)
        return readout(oT), (r, vT, kk, w, kd, ka, st)

    def bwd(res, do):
        r, vT, kk, w, kd, ka, st = res
        doT = jnp.stack([_to_feature_major(do[0]), _to_feature_major(do[1])])
        dr, dw, dkd, dkn, dka, dvT = _wkv_backward(r, vT, kk, w, kd, ka, st, doT, n_ctx)
        return dr[0] + dr[1], _from_feature_major(dvT[0] + dvT[1]), -(dkn[0] + dkn[1]), dw, dkd, dka

    op.defvjp(fwd, bwd)
    return op


GD_HB = 2


def _dot(a, b, dims=((1,), (0,))):
    return lax.dot_general(a, b, (dims, ((), ())), precision=HI, preferred_element_type=F32)


def _gdn_chunk(S, q, k, v, g, beta, rev):
    R = q.shape[0]
    C = CHUNK
    nh = R // C
    ri = lax.broadcasted_iota(jnp.int32, (R, R), 0)
    cj = lax.broadcasted_iota(jnp.int32, (R, R), 1)
    same = lax.div(ri, C) == lax.div(cj, C)
    lag = (ri - cj) * (1 - 2 * rev.astype(jnp.int32))
    incl = same & (lag >= 0)
    strict = same & (lag > 0)
    eye = (ri == cj).astype(F32)
    gb = jnp.broadcast_to(g, (R, GD_N))
    bb = jnp.broadcast_to(beta, (R, GD_N))
    G = _dot(incl.astype(F32), gb)
    Gc = jnp.concatenate([G] * (R // GD_N), axis=1) if R > GD_N else G[:, :R]
    Grow = _dot(jnp.ones((R, R), F32), eye * Gc)
    decay = jnp.where(incl, jnp.exp(jnp.where(incl, Gc - Grow, 0.0)), 0.0)
    kb = k * bb
    A = jnp.where(strict, _dot(kb, k, ((1,), (1,))) * decay, 0.0)
    Nk = -A
    T = eye + Nk
    for _ in range(int(math.log2(C)) - 1):
        Nk = _dot(Nk, Nk)
        T = T + _dot(T, Nk)
    u = _dot(T, v * bb)
    w = _dot(T, kb * jnp.exp(G))
    attn = jnp.where(incl, _dot(q, k, ((1,), (1,))) * decay, 0.0)
    head = lax.div(lax.broadcasted_iota(jnp.int32, (R, GD_N), 0), C)

    def spread(a):
        return jnp.concatenate([jnp.where(head == h, a, 0.0) for h in range(nh)], axis=1)

    v_new = u - _dot(spread(w), S)
    o = _dot(spread(q * jnp.exp(G)), S) + _dot(attn, v_new)
    Gtot = _dot(same.astype(F32), gb)
    k_dec = k * jnp.exp(Gtot - G)
    Gs = jnp.concatenate([Gtot[h * C:(h + 1) * C] for h in range(nh) for _ in range(GD_N // C)], axis=0)
    S_new = S * jnp.exp(Gs) + _dot(spread(k_dec), v_new, ((0,), (0,)))
    return S_new, o


def _stack_heads(ref):
    return jnp.concatenate([ref[:, h * GD_N:(h + 1) * GD_N] for h in range(GD_HB)], axis=0)


def _unstack_heads(ref, a):
    for h in range(GD_HB):
        ref[:, h * GD_N:(h + 1) * GD_N] = a[h * CHUNK:(h + 1) * CHUNK]


def _gdn_specs(n, n_ctx, back):
    nc, ncc = n // CHUNK, n_ctx // CHUNK

    def ci(d, c):
        return _chunk_index(d, nc - 1 - c if back else c, ncc, nc)

    tok = pl.BlockSpec((CHUNK, GD_HB * GD_N), lambda d, h, c: (ci(d, c), h))
    tok_d = pl.BlockSpec((None, CHUNK, GD_HB * GD_N), lambda d, h, c: (d, ci(d, c), h))
    col = pl.BlockSpec((None, GD_HB, CHUNK, 1), lambda d, h, c: (d, h, ci(d, c), 0))
    st = pl.BlockSpec((None, GD_HB, None, GD_N, GD_N), lambda d, h, c: (d, h, ci(d, c), 0, 0))
    return nc, tok, tok_d, col, st


def _ride(ride, bcast, first, last, refs_in, refs_out, sems):
    if not ride:
        return
    start, wait = _exchange_copies(refs_in, refs_out, [bcast] * len(ride), *sems)
    pl.when(first)(start)
    pl.when(last)(wait)


def _grid_ends(nc):
    d, h, c = pl.program_id(0), pl.program_id(1), pl.program_id(2)
    first = (d == 0) & (h == 0) & (c == 0)
    last = (d == 1) & (h == GD_H // GD_HB - 1) & (c == nc - 1)
    return first, last


def _gdn_forward(q, k, v, g, beta, n_ctx, ride=()):
    n = q.shape[0]
    nc, tok, tok_d, col, st = _gdn_specs(n, n_ctx, False)
    nr = len(ride)

    def body(*refs):
        q_ref, k_ref, v_ref, g_ref, b_ref = refs[:5]
        o_ref, st_ref = refs[5 + nr:7 + nr]
        S_ref = refs[7 + 2 * nr]
        first, last = _grid_ends(nc)
        _ride(ride, True, first, last, refs[5:5 + nr], refs[7 + nr:7 + 2 * nr], refs[8 + 2 * nr:])

        @pl.when(pl.program_id(2) == 0)
        def _():
            S_ref[...] = jnp.zeros_like(S_ref)

        S = S_ref[...]
        st_ref[...] = S.reshape(GD_HB, GD_N, GD_N)
        S_new, o = _gdn_chunk(S, _stack_heads(q_ref), _stack_heads(k_ref), _stack_heads(v_ref),
                              g_ref[...].reshape(GD_HB * CHUNK, 1), b_ref[...].reshape(GD_HB * CHUNK, 1),
                              pl.program_id(0) == 1)
        S_ref[...] = S_new
        _unstack_heads(o_ref, o)

    any_spec = pl.BlockSpec(memory_space=pl.ANY)
    res = pl.pallas_call(
        body, grid=(2, GD_H // GD_HB, nc), in_specs=[tok, tok, tok, col, col] + [any_spec] * nr,
        out_specs=[tok_d, st] + [any_spec] * nr,
        out_shape=[jax.ShapeDtypeStruct((2, n, GD_W), F32), jax.ShapeDtypeStruct((2, GD_H, nc, GD_N, GD_N), F32)]
        + _exchange_out_shapes(ride, [True] * nr),
        scratch_shapes=[pltpu.VMEM((GD_HB * GD_N, GD_N), F32)] + (_exchange_sems(nr) if nr else []),
        compiler_params=_cparams(("arbitrary", "arbitrary", "arbitrary")), name="gdn_fwd")(q, k, v, g, beta, *ride)
    return res[0], res[1], tuple(res[2:])


def _gdn_backward(q, k, v, g, beta, st, do, n_ctx, ride=()):
    n = q.shape[0]
    nc, tok, tok_d, col, st_spec = _gdn_specs(n, n_ctx, True)
    nr = len(ride)

    def body(*refs):
        q_ref, k_ref, v_ref, g_ref, b_ref, st_ref, do_ref = refs[:7]
        dq_ref, dk_ref, dv_ref, dg_ref, db_ref = refs[7 + nr:12 + nr]
        dS_ref = refs[12 + 2 * nr]
        first, last = _grid_ends(nc)
        _ride(ride, False, first, last, refs[7:7 + nr], refs[12 + nr:12 + 2 * nr], refs[13 + 2 * nr:])

        @pl.when(pl.program_id(2) == 0)
        def _():
            dS_ref[...] = jnp.zeros_like(dS_ref)

        rev = pl.program_id(0) == 1
        rows = GD_HB * CHUNK
        _, vjp = jax.vjp(lambda S, q_, k_, v_, g_, b_: _gdn_chunk(S, q_, k_, v_, g_, b_, rev),
                         st_ref[...].reshape(GD_HB * GD_N, GD_N), _stack_heads(q_ref), _stack_heads(k_ref),
                         _stack_heads(v_ref), g_ref[...].reshape(rows, 1), b_ref[...].reshape(rows, 1))
        dS, dq, dk, dv, dg, db = vjp((dS_ref[...], _stack_heads(do_ref)))
        dS_ref[...] = dS
        _unstack_heads(dq_ref, dq)
        _unstack_heads(dk_ref, dk)
        _unstack_heads(dv_ref, dv)
        dg_ref[...] = dg.reshape(GD_HB, CHUNK, 1)
        db_ref[...] = db.reshape(GD_HB, CHUNK, 1)

    tok_out = jax.ShapeDtypeStruct((2, n, GD_W), F32)
    col_out = jax.ShapeDtypeStruct((2, GD_H, n, 1), F32)
    any_spec = pl.BlockSpec(memory_space=pl.ANY)
    res = pl.pallas_call(
        body, grid=(2, GD_H // GD_HB, nc), in_specs=[tok, tok, tok, col, col, st_spec, tok_d] + [any_spec] * nr,
        out_specs=[tok_d, tok_d, tok_d, col, col] + [any_spec] * nr,
        out_shape=[tok_out, tok_out, tok_out, col_out, col_out] + _exchange_out_shapes(ride, [False] * nr),
        scratch_shapes=[pltpu.VMEM((GD_HB * GD_N, GD_N), F32)] + (_exchange_sems(nr) if nr else []),
        compiler_params=_cparams(("arbitrary", "arbitrary", "arbitrary")), name="gdn_bwd")(
            q, k, v, g, beta, st, do, *ride)
    return tuple(res[:5]), tuple(res[5:])


def _gdn_op(n_ctx):
    @jax.custom_vjp
    def op(q, k, v, g, beta, shards, sinks):
        o, _, gathered = _gdn_forward(q, k, v, g, beta, n_ctx, shards)
        return o, gathered, sinks

    def fwd(q, k, v, g, beta, shards, sinks):
        o, st, gathered = _gdn_forward(q, k, v, g, beta, n_ctx, shards)
        return (o, gathered, sinks), (q, k, v, g, beta, st, shards)

    def bwd(res, cts):
        q, k, v, g, beta, st, shards = res
        do, _, dsinks = cts
        (dq, dk, dv, dg, db), received = _gdn_backward(q, k, v, g, beta, st, do, n_ctx, tuple(dsinks))
        return dq[0] + dq[1], dk[0] + dk[1], dv[0] + dv[1], dg, db, tuple(None for _ in shards), received

    op.defvjp(fwd, bwd)
    return op


def _bdot(a, b):
    return jnp.dot(a.astype(BF16), b.astype(BF16), preferred_element_type=F32)


def _rms(x, eps=NORM_EPS):
    return x * lax.rsqrt(jnp.mean(x * x, axis=-1, keepdims=True) + eps)


def _softplus(x):
    return jnp.maximum(x, 0.0) + jnp.log(1.0 + jnp.exp(-jnp.abs(x)))


def _heads(width, seg):
    e = (lax.div(lax.broadcasted_iota(jnp.int32, (width, 128), 0), seg)
         == lax.broadcasted_iota(jnp.int32, (width, 128), 1)).astype(F32)
    et = (lax.div(lax.broadcasted_iota(jnp.int32, (128, width), 1), seg)
          == lax.broadcasted_iota(jnp.int32, (128, width), 0)).astype(F32)
    return e, et


def _head_sum(x, seg):
    e, et = _heads(x.shape[1], seg)
    return _dot(_dot(x, e), et)


def _l2n(x, seg):
    return x * lax.rsqrt(jnp.maximum(_head_sum(x * x, seg), 1e-12))


def _f_norm_mod(x, g, sc, sh):
    return ((_rms(x) * g) * (1.0 + sc) + sh,)


def _f_rw_prep(pm, k_k, k_a, w0, wup, a0, aup, gup):
    k = pm[:, RW_W:2 * RW_W]
    kk = _l2n(k * k_k, RW_N)
    ws, kds, kas = [], [], []
    for d in range(2):
        wd = pm[:, 3 * RW_W + 128 * d:3 * RW_W + 128 * (d + 1)]
        ad = pm[:, 3 * RW_W + 256 + 128 * d:3 * RW_W + 256 + 128 * (d + 1)]
        wlog = -_softplus(-(w0[d:d + 1] + _bdot(jnp.tanh(wd), wup[128 * d:128 * (d + 1)]))) - 0.5
        ws.append(jnp.exp(-jnp.exp(wlog)))
        a = jax.nn.sigmoid(a0[d:d + 1] + _bdot(ad, aup[128 * d:128 * (d + 1)]))
        kds.append(k * (1.0 + (a - 1.0) * k_a))
        kas.append(kk * a)
    g = _bdot(jax.nn.sigmoid(pm[:, 3 * RW_W + 512:3 * RW_W + 512 + GATE_RANK]), gup)
    return kk, ws[0], ws[1], kds[0], kds[1], kas[0], kas[1], g


def _f_rw_read(o0, o1, r, kd0, kd1, v, g, r_k, gn_g, gn_b):
    o = o0 + o1
    c = o - _head_sum(o, RW_N) * (1.0 / RW_N)
    var = _head_sum(c * c, RW_N) * (1.0 / RW_N)
    on = c * lax.rsqrt(var + RW_GN_EPS) * gn_g + gn_b
    bonus = _head_sum(r * (kd0 + kd1) * r_k, RW_N) * v
    return ((on + bonus) * g,)


def _f_gd_prep(cq, ab, alog, dtb):
    qkv = cq * jax.nn.sigmoid(cq)
    q = _l2n(qkv[:, :GD_W], GD_N) * (GD_N ** -0.5)
    k = _l2n(qkv[:, GD_W:2 * GD_W], GD_N)
    v = qkv[:, 2 * GD_W:]
    glog = -jnp.exp(alog) * _softplus(ab + dtb)
    lane = lax.broadcasted_iota(jnp.int32, ab.shape, 1)
    return q, k, v, jnp.where(lane < 2 * GD_H, glog, jax.nn.sigmoid(ab))


def _f_gd_read(o0, o1, z, ng):
    o = o0 + o1
    y = o * lax.rsqrt(_head_sum(o * o, GD_N) * (1.0 / GD_N) + NORM_EPS) * jnp.concatenate([ng] * GD_H, axis=1)
    return (y * (z * jax.nn.sigmoid(z)),)


def _f_merge(za, zb, gates):
    return (jax.nn.sigmoid(gates[:, :D]) * za + jax.nn.sigmoid(gates[:, D:]) * zb,)


def _f_res_norm(x, att, g1, ng, sc, sh):
    x1 = x + g1 * att
    return x1, (_rms(x1) * ng) * (1.0 + sc) + sh


def _f_glu(gc, val):
    return (0.5 * gc * (1.0 + lax.erf(gc * (2.0 ** -0.5))) * val,)


def _f_final(x1, ff, target, g2, fg):
    y = _rms(x1 + g2 * ff) * fg
    err = y - target
    return (jnp.broadcast_to(0.5 * jnp.mean(err * err, axis=-1, keepdims=True), (x1.shape[0], 128)),)


def _exchange_copies(ins, outs, bcast, send_sems, recv_sems, local_sems):
    n = len(ins)
    x, y, c = lax.axis_index("x"), lax.axis_index("y"), lax.axis_index("c")
    me = 4 * x + 2 * y + c
    own = [pltpu.make_async_copy(ins[i] if bcast[i] else ins[i].at[me], outs[i].at[me], local_sems.at[i])
           for i in range(n)]
    sends, landings = [], []
    for k in range(1, NDEV):
        kx, ky, kc = (k >> 2) & 1, (k >> 1) & 1, k & 1
        px, py, pc = (1 - x if kx else x), (1 - y if ky else y), (1 - c if kc else c)
        peer = 4 * px + 2 * py + pc
        for i in range(n):
            sem = i * (NDEV - 1) + k - 1

            def copy(dst_block, i=i, sem=sem, peer=peer, dev=(px, py, pc)):
                return pltpu.make_async_remote_copy(
                    src_ref=ins[i] if bcast[i] else ins[i].at[peer], dst_ref=outs[i].at[dst_block],
                    send_sem=send_sems.at[sem], recv_sem=recv_sems.at[sem],
                    device_id=dev, device_id_type=pl.DeviceIdType.MESH)

            sends.append(copy(me))
            landings.append(copy(peer))

    def start():
        for cp in own + sends:
            cp.start()

    def wait():
        for cp, landing in zip(sends, landings):
            cp.wait_send()
            landing.wait_recv()
        for cp in own:
            cp.wait()

    return start, wait


def _exchange_sems(n):
    return [pltpu.SemaphoreType.DMA((n * (NDEV - 1),)), pltpu.SemaphoreType.DMA((n * (NDEV - 1),)),
            pltpu.SemaphoreType.DMA((n,))]


def _exchange_out_shapes(arrays, bcast):
    return [jax.ShapeDtypeStruct((NDEV,) + (a.shape if b else a.shape[1:]), a.dtype) for a, b in zip(arrays, bcast)]


def _exchange(arrays, bcast, name):
    n = len(arrays)

    def body(*refs):
        start, wait = _exchange_copies(refs[:n], refs[n:2 * n], bcast, *refs[2 * n:])
        start()
        wait()

    any_spec = pl.BlockSpec(memory_space=pl.ANY)
    return pl.pallas_call(
        body, in_specs=[any_spec] * n, out_specs=[any_spec] * n, out_shape=_exchange_out_shapes(arrays, bcast),
        scratch_shapes=_exchange_sems(n),
        compiler_params=pltpu.CompilerParams(has_side_effects=True), name=name)(*arrays)


def _adamw(slabs, w, m, v, name):
    R, C = w.shape
    ns = slabs.shape[0]
    tr = _pick(R, (256, 128, 64, 32, 16, 8))

    def body(s_ref, w_ref, m_ref, v_ref, g_ref, d_ref, nm_ref, nv_ref):
        g = s_ref[0].astype(F32)
        for i in range(1, ns):
            g = g + s_ref[i].astype(F32)
        nm = ADAM_B1 * m_ref[...] + (1.0 - ADAM_B1) * g
        nv = ADAM_B2 * v_ref[...] + (1.0 - ADAM_B2) * (g * g)
        m_hat = nm / (1.0 - ADAM_B1 ** ADAM_STEP)
        v_hat = nv / (1.0 - ADAM_B2 ** ADAM_STEP)
        g_ref[...] = g
        d_ref[...] = -ADAM_LR * (m_hat / (jnp.sqrt(v_hat) + ADAM_EPS) + ADAM_WD * w_ref[...])
        nm_ref[...] = nm
        nv_ref[...] = nv

    blk = pl.BlockSpec((tr, C), lambda i: (i, 0))
    out = jax.ShapeDtypeStruct((R, C), F32)
    return pl.pallas_call(
        body, grid=(R // tr,), in_specs=[pl.BlockSpec((ns, tr, C), lambda i: (0, i, 0)), blk, blk, blk],
        out_specs=[blk] * 4, out_shape=[out] * 4, compiler_params=_cparams(("parallel",)), name=name)(slabs, w, m, v)


def _silu(z):
    return z * jax.nn.sigmoid(z)


def _ada_forward(c_ext, w, b):
    n = w.shape[1]
    tn = _pick(n, (512, 256, 128))

    def body(c_ref, w_ref, b_ref, o_ref):
        o_ref[...] = _bdot(_silu(c_ref[...]), w_ref[...]) + b_ref[...]

    return pl.pallas_call(
        body, grid=(n // tn,),
        in_specs=[pl.BlockSpec(c_ext.shape, lambda j: (0, 0)), pl.BlockSpec((D, tn), lambda j: (0, j)),
                  pl.BlockSpec((1, tn), lambda j: (0, j))],
        out_specs=pl.BlockSpec((16, tn), lambda j: (0, j)), out_shape=jax.ShapeDtypeStruct((16, n), F32),
        compiler_params=_cparams(("parallel",)), name="ada_fwd")(c_ext, w, b)


def _ada_dmod(lat, ctxr):
    n = lat.shape[1]
    tn = 2048

    def body(lat_ref, ctx_ref, o_ref):
        o_ref[...] = jnp.concatenate([lat_ref[...], jnp.broadcast_to(jnp.sum(ctx_ref[...], axis=0, keepdims=True), (8, tn))], axis=0)

    blk = pl.BlockSpec((NDEV, tn), lambda j: (0, j))
    return pl.pallas_call(
        body, grid=(n // tn,), in_specs=[blk, blk],
        out_specs=pl.BlockSpec((16, tn), lambda j: (0, j)), out_shape=jax.ShapeDtypeStruct((16, n), F32),
        compiler_params=_cparams(("parallel",)), name="ada_dmod")(lat, ctxr)


def _f_colsum16(dm):
    return (jnp.broadcast_to(jnp.sum(dm, axis=0, keepdims=True), dm.shape),)


def _f_silu_grad(dpart, c_ext):
    sg = jax.nn.sigmoid(c_ext)
    return (dpart * sg * (1.0 + c_ext * (1.0 - sg)),)


def _f_silu(c_ext):
    return (_silu(c_ext),)


def _total_forward(rows):
    n = rows.shape[0]
    tm = _pick(n, (256, 128, 64, 32, 16, 8))

    def body(r_ref, o_ref):
        @pl.when(pl.program_id(0) == 0)
        def _():
            o_ref[...] = jnp.zeros_like(o_ref)

        o_ref[...] += jnp.broadcast_to(jnp.sum(r_ref[...], axis=0, keepdims=True), (8, 128))

    return pl.pallas_call(
        body, grid=(n // tm,), in_specs=[pl.BlockSpec((tm, 128), lambda i: (i, 0))],
        out_specs=pl.BlockSpec((8, 128), lambda i: (0, 0)), out_shape=jax.ShapeDtypeStruct((8, 128), F32),
        compiler_params=_cparams(("arbitrary",)), name="loss_total")(rows)


@jax.custom_vjp
def _total(rows):
    return _total_forward(rows)[0, 0]


def _total_fwd(rows):
    return _total_forward(rows)[0, 0], rows


def _total_bwd(rows, ct):
    lane = lax.broadcasted_iota(jnp.int32, rows.shape, 1)
    return (jnp.where(lane == 0, ct, 0.0).astype(F32),)


_total.defvjp(_total_fwd, _total_bwd)


def _pad_cols(a, width):
    return jnp.pad(a, ((0, 0), (0, width - a.shape[1])))


def _rw_cols(a):
    parts = [a[:, :3 * RW_W]]
    for i in range(4):
        parts.append(_pad_cols(a[:, 3 * RW_W + RANK * i:3 * RW_W + RANK * (i + 1)], 128))
    parts.append(a[:, 3 * RW_W + 4 * RANK:])
    return jnp.concatenate(parts, axis=1)


RW_COLS = 3 * RW_W + 4 * RANK + GATE_RANK
GD_COLS = 4 * GD_W + 4 * GD_H


def _split_w_in(w):
    gd = w[:, RW_COLS:RW_COLS + GD_COLS]
    rw = jnp.concatenate([_rw_cols(w[:, :RW_COLS]), _pad_cols(gd[:, 4 * GD_W:], 256)], axis=1)
    return w[:, RW_COLS + GD_COLS:], rw, gd[:, :4 * GD_W]


def _unshard(g, col):
    if col:
        return jnp.swapaxes(g, 0, 1).reshape(g.shape[1], NDEV * g.shape[2])
    return g.reshape(NDEV * g.shape[1], g.shape[2])


def _reshard(a, col):
    if col:
        return jnp.swapaxes(a.reshape(a.shape[0], NDEV, a.shape[1] // NDEV), 0, 1)
    return a.reshape(NDEV, a.shape[0] // NDEV, a.shape[1])


def _pack(parts, width, row_mult):
    flat = [p.reshape(-1) for p in parts]
    offs, o = [], 0
    for f in flat:
        offs.append(o)
        o += f.shape[0]
    rows = -(-o // (width * row_mult)) * row_mult
    cat = jnp.concatenate(flat + [jnp.zeros((rows * width - o,), flat[0].dtype)])
    return cat.reshape(rows, width), offs


BIG = ("w_in", "w_a_out", "w_b_out", "w_o", "ffn_w1", "ffn_w2")
LATE = BIG[1:]
SMALL = ("rw_w0", "rw_w_up", "rw_a0", "rw_a_up", "rw_g_up", "gd_conv_w", "ffn_conv_w")
ROW_SHARDED = ("w_o", "ffn_w2")
REPL = ("norm1_g", "norm2_g", "rw_mu", "rw_k_k", "rw_k_a", "rw_r_k", "rw_gn_g", "rw_gn_b", "gd_a_log", "gd_dt_bias",
        "gd_norm_g", "final_norm_g")
WEIGHTS = ('c_ctx', 'w_ada', 'b_ada', 'norm1_g', 'norm2_g', 'w_in', 'rw_mu', 'rw_k_k', 'rw_k_a', 'rw_r_k', 'rw_w0', 'rw_w_up', 'rw_a0', 'rw_a_up', 'rw_g_up', 'rw_gn_g', 'rw_gn_b', 'gd_conv_w', 'gd_a_log', 'gd_dt_bias', 'gd_norm_g', 'w_a_out', 'w_b_out', 'w_o', 'ffn_w1', 'ffn_conv_w', 'ffn_w2', 'final_norm_g')


def _view2d(a):
    return a.reshape(-1, a.shape[-1])


def _layer_loss(x, modl, modc, sinks, small, repl, ctx, target, big, n_ctx):
    sh1, sc1, g1, sh2, sc2, g2 = [modl[:, i * D:(i + 1) * D] for i in range(6)]
    csh1, csc1 = modc[:, :D], modc[:, D:2 * D]
    n1g, n2g, fg = repl["norm1_g"], repl["norm2_g"], repl["final_norm_g"].reshape(1, D)
    (h_lat,) = _rowwise(_f_norm_mod, "norm1_lat", tm=256)((x,), (n1g, sc1, sh1))
    (h_ctx,) = _rowwise(_f_norm_mod, "norm1_ctx", tm=256)((ctx,), (n1g, csc1, csh1))
    h = jnp.concatenate([h_ctx, h_lat], axis=0)
    wg, wr, wd = _split_w_in(big["w_in"])
    sg, sr, sd = _split_w_in(_unshard(sinks["w_in"], True))
    p_gate = _dense("proj_gate")(h_lat, wg, sg)
    p_rw = _dense("proj_rw")(h, wr, sr)
    p_gd = _dense("proj_gd")(h, wd, sd)
    n = h.shape[0]
    lat = slice(n_ctx, n)

    mu = _rw_cols(repl["rw_mu"])
    coef = jnp.concatenate([0.5 * mu, 1.0 - mu, 0.5 * mu], axis=0)
    pm = _stencil("rw_shift", ("seg", (n_ctx,)), ncols=RW_COLS + 128, tm=256, tc=768)(p_rw, coef)
    ab = p_rw[:, RW_COLS + 128:RW_COLS + 256]
    pad_rank = lambda a: jnp.pad(a, ((0, 0), (0, 128 - RANK), (0, 0))).reshape(256, RW_W)
    kk, w0, w1, kd0, kd1, ka0, ka1, g = _rowwise(_f_rw_prep, "rw_prep", tm=128)(
        (pm,), (repl["rw_k_k"], repl["rw_k_a"], small["rw_w0"], pad_rank(small["rw_w_up"]), small["rw_a0"],
                pad_rank(small["rw_a_up"]), small["rw_g_up"]))
    r, v = pm[:, :RW_W], pm[:, 2 * RW_W:3 * RW_W]
    o = _wkv_op(n_ctx)(r, v, kk, jnp.stack([w0, w1]), jnp.stack([kd0, kd1]), jnp.stack([ka0, ka1]))
    (ya,) = _rowwise(_f_rw_read, "rw_read", tm=256)(
        (o[0, lat], o[1, lat], r[lat], kd0[lat], kd1[lat], v[lat], g[lat]),
        (repl["rw_r_k"], repl["rw_gn_g"], repl["rw_gn_b"]))

    cq = _stencil("gd_conv", ("seg", (n_ctx,)), ncols=3 * GD_W, tm=256, tc=768)(p_gd, small["gd_conv_w"])
    z = p_gd[lat, 3 * GD_W:]
    lanes16 = lambda a: _pad_cols(a.reshape(1, 2 * GD_H), 128)
    q, k, v2, gb = _rowwise(_f_gd_prep, "gd_prep", tm=256)(
        (cq, ab), (lanes16(repl["gd_a_log"]), lanes16(repl["gd_dt_bias"])))
    per_head = lambda a: a.T.reshape(2, GD_H, n, 1)
    og, gathered, late_sinks = _gdn_op(n_ctx)(
        q, k, v2, per_head(gb[:, :2 * GD_H]), per_head(gb[:, 2 * GD_H:4 * GD_H]),
        tuple(big[k_] for k_ in LATE), tuple(sinks[k_] for k_ in LATE))
    (yb,) = _rowwise(_f_gd_read, "gd_read", tm=256)((og[0, lat], og[1, lat], z), (repl["gd_norm_g"],))
    w = {k_: _unshard(g_, k_ not in ROW_SHARDED) for k_, g_ in zip(LATE, gathered)}
    s = {k_: _unshard(s_, k_ not in ROW_SHARDED) for k_, s_ in zip(LATE, late_sinks)}

    za = _dense("a_out")(ya, w["w_a_out"], s["w_a_out"])
    zb = _dense("b_out")(yb, w["w_b_out"], s["w_b_out"])
    (merged,) = _rowwise(_f_merge, "merge", tm=256)((za, zb, p_gate), ())
    att = _dense("w_o")(merged, w["w_o"], s["w_o"])
    x1, h2 = _rowwise(_f_res_norm, "res_norm2", tm=256)((x, att), (g1, n2g, sc2, sh2))
    ug = _dense("ffn_gate")(h2, w["ffn_w1"][:, :D_FF], s["ffn_w1"][:, :D_FF])
    uv = _dense("ffn_val")(h2, w["ffn_w1"][:, D_FF:], s["ffn_w1"][:, D_FF:])
    gc = _stencil("ffn_conv", ("grid", GRID_W), ncols=D_FF, tm=256, tc=1408)(ug, small["ffn_conv_w"])
    (act,) = _rowwise(_f_glu, "glu", tm=64)((gc, uv), ())
    ff = _dense("ffn_w2")(act, w["ffn_w2"], s["ffn_w2"])
    (rows,) = _rowwise(_f_final, "final", tm=256, n_nondiff=1)((x1, ff, target), (g2, fg))
    return _total(rows)


def kernel(x, c, ctx, c_ctx, w_ada, b_ada, norm1_g, norm2_g, w_in, rw_mu, rw_k_k, rw_k_a, rw_r_k, rw_w0, rw_w_up, rw_a0, rw_a_up, rw_g_up, rw_gn_g, rw_gn_b, gd_conv_w, gd_a_log, gd_dt_bias, gd_norm_g, w_a_out, w_b_out, w_o, ffn_w1, ffn_conv_w, ffn_w2, final_norm_g, loss_target, m_c_ctx, m_w_ada, m_b_ada, m_norm1_g, m_norm2_g, m_w_in, m_rw_mu, m_rw_k_k, m_rw_k_a, m_rw_r_k, m_rw_w0, m_rw_w_up, m_rw_a0, m_rw_a_up, m_rw_g_up, m_rw_gn_g, m_rw_gn_b, m_gd_conv_w, m_gd_a_log, m_gd_dt_bias, m_gd_norm_g, m_w_a_out, m_w_b_out, m_w_o, m_ffn_w1, m_ffn_conv_w, m_ffn_w2, m_final_norm_g, v_c_ctx, v_w_ada, v_b_ada, v_norm1_g, v_norm2_g, v_w_in, v_rw_mu, v_rw_k_k, v_rw_k_a, v_rw_r_k, v_rw_w0, v_rw_w_up, v_rw_a0, v_rw_a_up, v_rw_g_up, v_rw_gn_g, v_rw_gn_b, v_gd_conv_w, v_gd_a_log, v_gd_dt_bias, v_gd_norm_g, v_w_a_out, v_w_b_out, v_w_o, v_ffn_w1, v_ffn_conv_w, v_ffn_w2, v_final_norm_g):
    args = dict(x=x, c=c, ctx=ctx, c_ctx=c_ctx, w_ada=w_ada, b_ada=b_ada, norm1_g=norm1_g, norm2_g=norm2_g, w_in=w_in, rw_mu=rw_mu, rw_k_k=rw_k_k, rw_k_a=rw_k_a, rw_r_k=rw_r_k, rw_w0=rw_w0, rw_w_up=rw_w_up, rw_a0=rw_a0, rw_a_up=rw_a_up, rw_g_up=rw_g_up, rw_gn_g=rw_gn_g, rw_gn_b=rw_gn_b, gd_conv_w=gd_conv_w, gd_a_log=gd_a_log, gd_dt_bias=gd_dt_bias, gd_norm_g=gd_norm_g, w_a_out=w_a_out, w_b_out=w_b_out, w_o=w_o, ffn_w1=ffn_w1, ffn_conv_w=ffn_conv_w, ffn_w2=ffn_w2, final_norm_g=final_norm_g, loss_target=loss_target, m_c_ctx=m_c_ctx, m_w_ada=m_w_ada, m_b_ada=m_b_ada, m_norm1_g=m_norm1_g, m_norm2_g=m_norm2_g, m_w_in=m_w_in, m_rw_mu=m_rw_mu, m_rw_k_k=m_rw_k_k, m_rw_k_a=m_rw_k_a, m_rw_r_k=m_rw_r_k, m_rw_w0=m_rw_w0, m_rw_w_up=m_rw_w_up, m_rw_a0=m_rw_a0, m_rw_a_up=m_rw_a_up, m_rw_g_up=m_rw_g_up, m_rw_gn_g=m_rw_gn_g, m_rw_gn_b=m_rw_gn_b, m_gd_conv_w=m_gd_conv_w, m_gd_a_log=m_gd_a_log, m_gd_dt_bias=m_gd_dt_bias, m_gd_norm_g=m_gd_norm_g, m_w_a_out=m_w_a_out, m_w_b_out=m_w_b_out, m_w_o=m_w_o, m_ffn_w1=m_ffn_w1, m_ffn_conv_w=m_ffn_conv_w, m_ffn_w2=m_ffn_w2, m_final_norm_g=m_final_norm_g, v_c_ctx=v_c_ctx, v_w_ada=v_w_ada, v_b_ada=v_b_ada, v_norm1_g=v_norm1_g, v_norm2_g=v_norm2_g, v_w_in=v_w_in, v_rw_mu=v_rw_mu, v_rw_k_k=v_rw_k_k, v_rw_k_a=v_rw_k_a, v_rw_r_k=v_rw_r_k, v_rw_w0=v_rw_w0, v_rw_w_up=v_rw_w_up, v_rw_a0=v_rw_a0, v_rw_a_up=v_rw_a_up, v_rw_g_up=v_rw_g_up, v_rw_gn_g=v_rw_gn_g, v_rw_gn_b=v_rw_gn_b, v_gd_conv_w=v_gd_conv_w, v_gd_a_log=v_gd_a_log, v_gd_dt_bias=v_gd_dt_bias, v_gd_norm_g=v_gd_norm_g, v_w_a_out=v_w_a_out, v_w_b_out=v_w_b_out, v_w_o=v_w_o, v_ffn_w1=v_ffn_w1, v_ffn_conv_w=v_ffn_conv_w, v_ffn_w2=v_ffn_w2, v_final_norm_g=v_final_norm_g)
    me = 4 * lax.axis_index("x") + 2 * lax.axis_index("y") + lax.axis_index("c")
    x2, ctx2, target = args["x"][0], args["ctx"][0], args["loss_target"][0]
    shard = {k: _view2d(args[k][0]) for k in BIG + SMALL}

    small_pack, small_offs = _pack([args["c"]] + [shard[k] for k in SMALL], 128, 8)
    gathered = _exchange([shard["w_in"].astype(BF16), small_pack], [True, True], "gather_weights")
    small_all = gathered[-1].reshape(NDEV, -1)

    def unpack(allg, off, k):
        r_, c_ = shard[k].shape
        return _unshard(allg[:, off:off + r_ * c_].reshape(NDEV, r_, c_), k not in ROW_SHARDED)

    big = {k: shard[k].astype(BF16) for k in LATE}
    big["w_in"] = _unshard(gathered[0], True)
    small = {k: unpack(small_all, o, k) for k, o in zip(SMALL, small_offs[1:])}
    small["rw_w_up"] = small["rw_w_up"].reshape(2, RANK, RW_W)
    small["rw_a_up"] = small["rw_a_up"].reshape(2, RANK, RW_W)
    c_all = small_all[:, :D]

    c_ext = jnp.concatenate([c_all, args["c_ctx"].reshape(1, D), jnp.zeros((7, D), F32)], axis=0)
    w_ada = args["w_ada"][0]
    nb = w_ada.shape[1]
    mod_blk = _ada_forward(c_ext, w_ada, lax.dynamic_slice(args["b_ada"], (0, me * nb), (1, nb)))
    (mod_all,) = _exchange([mod_blk], [True], "gather_mod")
    mod_all = jnp.swapaxes(mod_all, 0, 1).reshape(16, NDEV * nb)
    modl = lax.dynamic_slice(mod_all, (me, 0), (1, NDEV * nb))
    modc = mod_all[NDEV:NDEV + 1]

    repl = {k: args[k] for k in REPL}
    sinks = {k: jnp.zeros((NDEV,) + shard[k].shape, GRAD_WIRE) for k in BIG}
    n_ctx = ctx2.shape[0]
    loss, grads = jax.value_and_grad(_layer_loss, argnums=(0, 1, 2, 3, 4, 5))(
        x2, modl, modc, sinks, small, repl, ctx2, target, big, n_ctx)
    dx, dmodl, dmodc, dbig, dsmall, drepl = grads
    loss = lax.psum(loss, ("x", "y", "c"))

    rep_pack, rep_offs = _pack([dmodl, dmodc] + [drepl[k] for k in REPL], 128, 8)
    (rep_all,) = _exchange([rep_pack], [True], "gather_small_grads")
    rep_all = rep_all.reshape(NDEV, -1)
    six_d = NDEV * nb
    dmod_ext = _ada_dmod(rep_all[:, :six_d], rep_all[:, six_d:2 * six_d])
    dmod_blk = lax.dynamic_slice(dmod_ext, (0, me * nb), (16, nb))
    (s_ext,) = _rowwise(_f_silu, "ada_silu", tm=16)((c_ext,), ())
    g_w_ada = _mm(s_ext, dmod_blk, ta=True, name="ada_dw")
    dpart = _mm(dmod_blk, w_ada, tb=True, name="ada_dc")
    (dpart,) = _rowwise(_f_silu_grad, "ada_dsilu", tm=16)((dpart, c_ext), ())

    flat_small = [_reshard(_view2d(dsmall[k]) if k != "ffn_conv_w" else dsmall[k], True).reshape(NDEV, -1) for k in SMALL]
    small_send = jnp.concatenate(flat_small, axis=1)
    pad = -small_send.shape[1] % 1024
    small_send = jnp.pad(small_send, ((0, 0), (0, pad))).reshape(NDEV, -1, 128)
    got = _exchange([dbig["w_in"], small_send, dpart[NDEV:NDEV + 1]], [False, False, True], "scatter_grads")
    small_got = got[1].reshape(NDEV, -1)

    slabs = {k: dbig[k] for k in LATE}
    slabs["w_in"] = got[0]
    o = 0
    for k in SMALL:
        r_, c_ = shard[k].shape
        slabs[k] = small_got[:, o:o + r_ * c_].reshape(NDEV, r_, c_)
        o += r_ * c_
    slabs["w_ada"] = g_w_ada[None]
    slabs["c_ctx"] = got[2]
    slabs["b_ada"] = jnp.concatenate([rep_all[:, :six_d], rep_all[:, six_d:2 * six_d]], axis=0)[:, None, :]
    for k, off in zip(REPL, rep_offs[2:]):
        size = args[k].size
        slabs[k] = rep_all[:, off:off + size].reshape((NDEV,) + _view2d(args[k]).shape)

    outs = {}
    for k in WEIGHTS:
        shape = args[k].shape
        res = _adamw(slabs[k], _view2d(args[k]), _view2d(args["m_" + k]), _view2d(args["v_" + k]), "adamw_" + k)
        outs[k] = [a.reshape(shape) for a in res]
    return (loss, dx[None]) + tuple(outs[k][i] for i in range(4) for k in WEIGHTS)
```

```python
import functools
import math

import jax
import jax.numpy as jnp
from jax import lax
from jax.experimental import pallas as pl
from jax.experimental.pallas import tpu as pltpu

F32 = jnp.float32
BF16 = jnp.bfloat16
HI = lax.Precision.HIGHEST
GRAD_WIRE = BF16

D = 2048
NCTX = 256
GRID_W = 64
NORM_EPS = 1e-6
RW_H, RW_N = 16, 64
RW_W = RW_H * RW_N
RANK = 96
GATE_RANK = 256
RW_GN_EPS = 64e-5
GD_H, GD_N = 8, 128
GD_W = GD_H * GD_N
CHUNK = 64
D_FF = 5632
NDEV = 8
PW = 4096
VMEM_LIMIT = 56 * 1024 * 1024
MM_VMEM_BUDGET = 46 * 1024 * 1024

ADAM_LR, ADAM_B1, ADAM_B2, ADAM_EPS, ADAM_WD, ADAM_STEP = 0.001, 0.9, 0.999, 1e-08, 0.01, 10


def _cparams(sem):
    return pltpu.CompilerParams(dimension_semantics=sem, vmem_limit_bytes=VMEM_LIMIT)


def _pick(n, cands):
    for c in cands:
        if n % c == 0:
            return c
    return n


def _mm(a, b, *, ta=False, tb=False, name, out_dtype=F32):
    M = a.shape[1] if ta else a.shape[0]
    K = a.shape[0] if ta else a.shape[1]
    N = b.shape[0] if tb else b.shape[1]
    assert K == (b.shape[1] if tb else b.shape[0])
    tm = _pick(M, (512, 768, 256, 128, 64, 16, 8))

    def vmem_bytes(tn_):
        return (2 * tm * K * a.dtype.itemsize + tm * K * 2 + 2 * tn_ * K * b.dtype.itemsize
                + 2 * tm * tn_ * jnp.dtype(out_dtype).itemsize)

    tn = next(t for t in (1024, 512, 256, 128) if N % t == 0 and (vmem_bytes(t) <= MM_VMEM_BUDGET or t == 128))
    dn = (((0 if ta else 1,), (1 if tb else 0,)), ((), ()))

    def body(a_ref, b_ref, o_ref, a16):
        @pl.when(pl.program_id(1) == 0)
        def _():
            a16[...] = a_ref[...].astype(BF16)

        o_ref[...] = lax.dot_general(a16[...], b_ref[...].astype(BF16), dn,
                                     preferred_element_type=F32).astype(out_dtype)

    a_blk = (K, tm) if ta else (tm, K)
    a_spec = pl.BlockSpec(a_blk, (lambda i, j: (0, i)) if ta else (lambda i, j: (i, 0)))
    b_spec = pl.BlockSpec((tn, K), lambda i, j: (j, 0)) if tb else pl.BlockSpec((K, tn), lambda i, j: (0, j))
    return pl.pallas_call(
        body, grid=(M // tm, N // tn), in_specs=[a_spec, b_spec],
        out_specs=pl.BlockSpec((tm, tn), lambda i, j: (i, j)),
        out_shape=jax.ShapeDtypeStruct((M, N), out_dtype),
        scratch_shapes=[pltpu.VMEM(a_blk, BF16)],
        compiler_params=_cparams(("parallel", "arbitrary")), name=name)(a, b)


def _dense(name):
    @jax.custom_vjp
    def f(a, w, sink):
        return _mm(a, w, name=name + "_fwd")

    def fwd(a, w, sink):
        return _mm(a, w, name=name + "_fwd"), (a, w)

    def bwd(res, dc):
        a, w = res
        return (_mm(dc, w, tb=True, name=name + "_da"), None,
                _mm(a, dc, ta=True, name=name + "_dw", out_dtype=GRAD_WIRE))

    f.defvjp(fwd, bwd)
    return f


def _rowwise(fn, name, *, tm, n_nondiff=0):
    def shapes(rows, params):
        tiles = [jax.ShapeDtypeStruct((tm, r.shape[1]), r.dtype) for r in rows]
        ps = [jax.ShapeDtypeStruct(p.shape, p.dtype) for p in params]
        return tiles, ps, jax.eval_shape(fn, *tiles, *ps)

    def row_spec(width):
        return pl.BlockSpec((tm, width), lambda i: (i, 0))

    def whole_spec(shape):
        return pl.BlockSpec(shape, lambda i: (0,) * len(shape))

    def forward(rows, params):
        n = rows[0].shape[0]
        tiles, ps, outs = shapes(rows, params)
        nin = len(rows) + len(params)

        def body(*refs):
            res = fn(*[r[...] for r in refs[:nin]])
            for o_ref, v in zip(refs[nin:], res):
                o_ref[...] = v.astype(o_ref.dtype)

        return pl.pallas_call(
            body, grid=(n // tm,),
            in_specs=[row_spec(t.shape[1]) for t in tiles] + [whole_spec(p.shape) for p in ps],
            out_specs=[row_spec(o.shape[1]) for o in outs],
            out_shape=[jax.ShapeDtypeStruct((n, o.shape[1]), o.dtype) for o in outs],
            compiler_params=_cparams(("parallel",)), name=name + "_fwd")(*rows, *params)

    def backward(rows, params, cts):
        n = rows[0].shape[0]
        tiles, ps, outs = shapes(rows, params)
        nr, npar, nout = len(rows), len(params), len(outs)
        nd = nr - n_nondiff

        def body(*refs):
            ins = [r[...] for r in refs[:nr + npar]]
            ct = tuple(r[...] for r in refs[nr + npar:nr + npar + nout])
            out_refs = refs[nr + npar + nout:]
            fixed = ins[nd:nr]

            def g(*diff):
                return fn(*diff[:nd], *fixed, *diff[nd:])

            _, vjp = jax.vjp(g, *ins[:nd], *ins[nr:])
            grads = vjp(ct)
            for o_ref, v in zip(out_refs[:nd], grads[:nd]):
                o_ref[...] = v.astype(o_ref.dtype)

            @pl.when(pl.program_id(0) == 0)
            def _():
                for o_ref in out_refs[nd:]:
                    o_ref[...] = jnp.zeros_like(o_ref)

            for o_ref, v in zip(out_refs[nd:], grads[nd:]):
                o_ref[...] += v

        res = pl.pallas_call(
            body, grid=(n // tm,),
            in_specs=[row_spec(t.shape[1]) for t in tiles] + [whole_spec(p.shape) for p in ps]
            + [row_spec(o.shape[1]) for o in outs],
            out_specs=[row_spec(t.shape[1]) for t in tiles[:nd]] + [whole_spec(p.shape) for p in ps],
            out_shape=[jax.ShapeDtypeStruct((n, t.shape[1]), t.dtype) for t in tiles[:nd]]
            + [jax.ShapeDtypeStruct(p.shape, p.dtype) for p in ps],
            compiler_params=_cparams(("arbitrary",)), name=name + "_bwd")(*rows, *params, *cts)
        return tuple(res[:nd]) + (None,) * n_nondiff, tuple(res[nd:])

    @jax.custom_vjp
    def op(rows, params):
        return tuple(forward(rows, params))

    def op_fwd(rows, params):
        return tuple(forward(rows, params)), (rows, params)

    def op_bwd(res, cts):
        return backward(res[0], res[1], cts)

    op.defvjp(op_fwd, op_bwd)
    return op


def _valid(t, off, n, mode):
    s = t + off[0]
    ok = (s >= 0) & (s < n)
    if mode[0] == "seg":
        for b in mode[1]:
            ok = ok & ((t >= b) == (s >= b))
    else:
        col = lax.rem(t, mode[1]) + off[1]
        ok = ok & (col >= 0) & (col < mode[1])
    return ok


def _stencil_offsets(mode):
    if mode[0] == "seg":
        return [(-1, 0), (0, 0), (1, 0)]
    w = mode[1]
    return [(di * w + dj, dj) for di in (-1, 0, 1) for dj in (-1, 0, 1)]


def _stencil(name, mode, *, ncols, tm, tc):
    offs = _stencil_offsets(mode)
    J = len(offs)
    halo = 8 if mode[0] == "seg" else 128

    def x_specs(n):
        nb = n // halo
        r = tm // halo
        return [pl.BlockSpec((halo, tc), lambda i, j: (jnp.maximum(i * r - 1, 0), j)),
                pl.BlockSpec((tm, tc), lambda i, j: (i, j)),
                pl.BlockSpec((halo, tc), lambda i, j: (jnp.minimum((i + 1) * r, nb - 1), j))]

    def fill(buf, prev, cur, nxt):
        buf[0:halo, :] = prev[...]
        buf[halo:halo + tm, :] = cur[...]
        buf[halo + tm:, :] = nxt[...]

    def forward(x, coef, suffix=""):
        n = x.shape[0]

        def body(prev, cur, nxt, c_ref, o_ref, buf):
            fill(buf, prev, cur, nxt)
            t = pl.program_id(0) * tm + lax.broadcasted_iota(jnp.int32, (tm, 1), 0)
            acc = jnp.zeros((tm, tc), F32)
            for j, off in enumerate(offs):
                xs = buf[halo + off[0]:halo + off[0] + tm, :]
                acc = acc + jnp.where(_valid(t, off, n, mode), xs, 0.0) * c_ref[j:j + 1, :]
            o_ref[...] = acc

        return pl.pallas_call(
            body, grid=(n // tm, ncols // tc),
            in_specs=x_specs(n) + [pl.BlockSpec((J, tc), lambda i, j: (0, j))],
            out_specs=pl.BlockSpec((tm, tc), lambda i, j: (i, j)),
            out_shape=jax.ShapeDtypeStruct((n, ncols), F32),
            scratch_shapes=[pltpu.VMEM((tm + 2 * halo, tc), F32)],
            compiler_params=_cparams(("parallel", "parallel")), name=name + suffix)(x, x, x, coef)

    def wgrad(x, dy):
        n = x.shape[0]

        def body(prev, cur, nxt, dy_ref, o_ref, buf):
            fill(buf, prev, cur, nxt)
            i = pl.program_id(1)
            t = i * tm + lax.broadcasted_iota(jnp.int32, (tm, 1), 0)

            @pl.when(i == 0)
            def _():
                o_ref[...] = jnp.zeros_like(o_ref)

            dy = dy_ref[...]
            for j, off in enumerate(offs):
                xs = buf[halo + off[0]:halo + off[0] + tm, :]
                o_ref[j:j + 1, :] += jnp.sum(jnp.where(_valid(t, off, n, mode), xs, 0.0) * dy, axis=0, keepdims=True)

        specs = [pl.BlockSpec(s.block_shape, (lambda f: lambda j, i: f(i, j))(s.index_map)) for s in x_specs(n)]
        return pl.pallas_call(
            body, grid=(ncols // tc, n // tm),
            in_specs=specs + [pl.BlockSpec((tm, tc), lambda j, i: (i, j))],
            out_specs=pl.BlockSpec((J, tc), lambda j, i: (0, j)),
            out_shape=jax.ShapeDtypeStruct((J, ncols), F32),
            scratch_shapes=[pltpu.VMEM((tm + 2 * halo, tc), F32)],
            compiler_params=_cparams(("parallel", "arbitrary")), name=name + "_wgrad")(x, x, x, dy)

    @jax.custom_vjp
    def op(x, coef):
        return forward(x, coef)

    def op_fwd(x, coef):
        return forward(x, coef), (x, coef)

    def op_bwd(res, dy):
        x, coef = res
        dx = forward(dy, coef[::-1], "_adj")
        if x.shape[1] != ncols:
            dx = jnp.pad(dx, ((0, 0), (0, x.shape[1] - ncols)))
        return dx, wgrad(x, dy)

    op.defvjp(op_fwd, op_bwd)
    return op


WKV_TC = 32
WKV_TB = 64


def _chunk_index(d, c, n_ctx_chunks, n_chunks):
    rev = jnp.where(c < n_ctx_chunks, n_ctx_chunks - 1 - c, n_ctx_chunks + n_chunks - 1 - c)
    return jnp.where(d == 0, c, rev)


def _to_feature_major(a):
    n = a.shape[0]
    t = a.reshape(n // WKV_TB, WKV_TB, RW_H // 2, 2, RW_N).transpose(2, 4, 0, 3, 1)
    return t.reshape(RW_W // 2, n // WKV_TB * 128)


def _from_feature_major(t):
    n = t.shape[1] // 128 * WKV_TB
    a = t.reshape(RW_H // 2, RW_N, n // WKV_TB, 2, WKV_TB).transpose(2, 4, 0, 3, 1)
    return a.reshape(n, RW_W)


def _pair_consts():
    lane = lax.broadcasted_iota(jnp.int32, (1, 128), 1)
    first = lax.broadcasted_iota(jnp.int32, (RW_N, 128), 1) < RW_N
    same = (lax.div(lax.broadcasted_iota(jnp.int32, (128, 128), 0), RW_N)
            == lax.div(lax.broadcasted_iota(jnp.int32, (128, 128), 1), RW_N))
    return lane < RW_N, lane >= RW_N, first, same.astype(BF16)


def _seg_reduce(x, row, m0, m1, first):
    s0 = jnp.sum(x * jnp.where(m0, row, 0.0), axis=1, keepdims=True)
    s1 = jnp.sum(x * jnp.where(m1, row, 0.0), axis=1, keepdims=True)
    return jnp.where(first, s0, s1)


def _seg_sum_mxu(x, same, passes):
    acc, rest = None, x
    for i in range(passes):
        piece = rest.astype(BF16)
        part = jnp.dot(piece, same, preferred_element_type=F32)
        acc = part if acc is None else acc + part
        if i + 1 < passes:
            rest = rest - piece.astype(F32)
    return acc


def _wkv_forward(r, vT, kk, w, kd, ka, n_ctx):
    n = r.shape[0]
    nc, ncc = n // WKV_TC, n_ctx // WKV_TC
    per_blk = WKV_TB // WKV_TC

    def body(r_ref, vT_ref, kk_ref, w_ref, kd_ref, ka_ref, oT_ref, st_ref, S_ref):
        d, c = pl.program_id(0), pl.program_id(1)
        ci = _chunk_index(d, c, ncc, nc)

        @pl.when(c == 0)
        def _():
            S_ref[...] = jnp.zeros_like(S_ref)

        @pl.when(lax.rem(c, per_blk) == 0)
        def _():
            oT_ref[...] = jnp.zeros_like(oT_ref)

        m0, m1, first, same = _pair_consts()
        lane_t = lax.rem(lax.broadcasted_iota(jnp.int32, (RW_N, 128), 1), WKV_TB)

        def step(s, carry):
            tl = jnp.where(d == 0, s, WKV_TC - 1 - s)
            sel = lane_t == lax.rem(ci, per_blk) * WKV_TC + tl
            row = pl.ds(tl, 1)
            kn_all, w_all, kd_all, ka_all, r_all = -kk_ref[row, :], w_ref[row, :], kd_ref[row, :], ka_ref[row, :], r_ref[row, :]
            pairs = range(RW_H // 2)
            cs = [slice(p * 128, (p + 1) * 128) for p in pairs]
            rows = [slice(p * RW_N, (p + 1) * RW_N) for p in pairs]
            S = [S_ref[p] for p in pairs]
            for p in pairs:
                st_ref[tl, p] = S[p]
            sab = [_seg_reduce(S[p], kn_all[:, cs[p]], m0, m1, first) for p in pairs]
            vb = [_seg_sum_mxu(jnp.where(sel, vT_ref[rows[p], :], 0.0), same, 2) for p in pairs]
            S = [S[p] * w_all[:, cs[p]] + sab[p] * ka_all[:, cs[p]] + vb[p] * kd_all[:, cs[p]] for p in pairs]
            for p in pairs:
                S_ref[p] = S[p]
            for p in pairs:
                ob = _seg_reduce(S[p], r_all[:, cs[p]], m0, m1, first)
                oT_ref[rows[p], :] = jnp.where(sel, ob, oT_ref[rows[p], :])
            return carry

        lax.fori_loop(0, WKV_TC, step, 0, unroll=4)

    def row(dirn):
        if dirn:
            return pl.BlockSpec((None, WKV_TC, RW_W), lambda d, c: (d, _chunk_index(d, c, ncc, nc), 0))
        return pl.BlockSpec((WKV_TC, RW_W), lambda d, c: (_chunk_index(d, c, ncc, nc), 0))

    colT = pl.BlockSpec((RW_W // 2, 128), lambda d, c: (0, lax.div(_chunk_index(d, c, ncc, nc), per_blk)))
    return pl.pallas_call(
        body, grid=(2, nc),
        in_specs=[row(0), colT, row(0), row(1), row(1), row(1)],
        out_specs=[pl.BlockSpec((None, RW_W // 2, 128), lambda d, c: (d, 0, lax.div(_chunk_index(d, c, ncc, nc), per_blk))),
                   pl.BlockSpec((None, WKV_TC, RW_H // 2, RW_N, 128),
                                lambda d, c: (d, _chunk_index(d, c, ncc, nc), 0, 0, 0))],
        out_shape=[jax.ShapeDtypeStruct((2, RW_W // 2, n // WKV_TB * 128), F32),
                   jax.ShapeDtypeStruct((2, n, RW_H // 2, RW_N, 128), F32)],
        scratch_shapes=[pltpu.VMEM((RW_H // 2, RW_N, 128), F32)],
        compiler_params=_cparams(("arbitrary", "arbitrary")), name="wkv_fwd")(r, vT, kk, w, kd, ka)


def _wkv_backward(r, vT, kk, w, kd, ka, st, doT, n_ctx):
    n = r.shape[0]
    nc, ncc = n // WKV_TC, n_ctx // WKV_TC
    per_blk = WKV_TB // WKV_TC

    def body(r_ref, vT_ref, kk_ref, w_ref, kd_ref, ka_ref, st_ref, doT_ref,
             dr_ref, dw_ref, dkd_ref, dkn_ref, dka_ref, dvT_ref, dS_ref):
        d, c = pl.program_id(0), pl.program_id(1)
        ci = _chunk_index(d, nc - 1 - c, ncc, nc)

        @pl.when(c == 0)
        def _():
            dS_ref[...] = jnp.zeros_like(dS_ref)

        @pl.when(lax.rem(c, per_blk) == 0)
        def _():
            dvT_ref[...] = jnp.zeros_like(dvT_ref)

        m0, m1, first, same = _pair_consts()
        lane_t = lax.rem(lax.broadcasted_iota(jnp.int32, (RW_N, 128), 1), WKV_TB)

        def step(s, carry):
            tl = jnp.where(d == 0, WKV_TC - 1 - s, s)
            sel = lane_t == lax.rem(ci, per_blk) * WKV_TC + tl
            row = pl.ds(tl, 1)
            kn_all, w_all, kd_all, ka_all, r_all = -kk_ref[row, :], w_ref[row, :], kd_ref[row, :], ka_ref[row, :], r_ref[row, :]
            pairs = range(RW_H // 2)
            cs = [slice(p * 128, (p + 1) * 128) for p in pairs]
            rows = [slice(p * RW_N, (p + 1) * RW_N) for p in pairs]
            colsum = lambda a: jnp.sum(a, axis=0, keepdims=True)
            Sp = [st_ref[tl, p] for p in pairs]
            dob = [_seg_sum_mxu(jnp.where(sel, doT_ref[rows[p], :], 0.0), same, 2) for p in pairs]
            dS = [dS_ref[p] + dob[p] * r_all[:, cs[p]] for p in pairs]
            dsab = [_seg_reduce(dS[p], ka_all[:, cs[p]], m0, m1, first) for p in pairs]
            sab = [_seg_reduce(Sp[p], kn_all[:, cs[p]], m0, m1, first) for p in pairs]
            vb = [_seg_sum_mxu(jnp.where(sel, vT_ref[rows[p], :], 0.0), same, 2) for p in pairs]
            St = [Sp[p] * w_all[:, cs[p]] + sab[p] * ka_all[:, cs[p]] + vb[p] * kd_all[:, cs[p]] for p in pairs]
            dr_ref[row, :] = jnp.concatenate([colsum(St[p] * dob[p]) for p in pairs], axis=1)
            dw_ref[row, :] = jnp.concatenate([colsum(dS[p] * Sp[p]) for p in pairs], axis=1)
            dka_ref[row, :] = jnp.concatenate([colsum(dS[p] * sab[p]) for p in pairs], axis=1)
            dkd_ref[row, :] = jnp.concatenate([colsum(dS[p] * vb[p]) for p in pairs], axis=1)
            dkn_ref[row, :] = jnp.concatenate([colsum(Sp[p] * dsab[p]) for p in pairs], axis=1)
            for p in pairs:
                dvb = _seg_reduce(dS[p], kd_all[:, cs[p]], m0, m1, first)
                dvT_ref[rows[p], :] = jnp.where(sel, dvb, dvT_ref[rows[p], :])
                dS_ref[p] = dS[p] * w_all[:, cs[p]] + dsab[p] * kn_all[:, cs[p]]
            return carry

        lax.fori_loop(0, WKV_TC, step, 0, unroll=4)

    def cidx(d, c):
        return _chunk_index(d, nc - 1 - c, ncc, nc)

    def row(dirn):
        if dirn:
            return pl.BlockSpec((None, WKV_TC, RW_W), lambda d, c: (d, cidx(d, c), 0))
        return pl.BlockSpec((WKV_TC, RW_W), lambda d, c: (cidx(d, c), 0))

    colT = pl.BlockSpec((RW_W // 2, 128), lambda d, c: (0, lax.div(cidx(d, c), per_blk)))
    colT_d = pl.BlockSpec((None, RW_W // 2, 128), lambda d, c: (d, 0, lax.div(cidx(d, c), per_blk)))
    st_spec = pl.BlockSpec((None, WKV_TC, RW_H // 2, RW_N, 128), lambda d, c: (d, cidx(d, c), 0, 0, 0))
    rows_out = jax.ShapeDtypeStruct((2, n, RW_W), F32)
    return pl.pallas_call(
        body, grid=(2, nc),
        in_specs=[row(0), colT, row(0), row(1), row(1), row(1), st_spec, colT_d],
        out_specs=[row(1)] * 5 + [colT_d],
        out_shape=[rows_out] * 5 + [jax.ShapeDtypeStruct((2, RW_W // 2, n // WKV_TB * 128), F32)],
        scratch_shapes=[pltpu.VMEM((RW_H // 2, RW_N, 128), F32)],
        compiler_params=_cparams(("arbitrary", "arbitrary")), name="wkv_bwd")(r, vT, kk, w, kd, ka, st, doT)


def _wkv_op(n_ctx):
    def readout(oT):
        return jnp.stack([_from_feature_major(oT[0]), _from_feature_major(oT[1])])

    @jax.custom_vjp
    def op(r, v, kk, w, kd, ka):
        return readout(_wkv_forward(r, _to_feature_major(v), kk, w, kd, ka, n_ctx)[0])

    def fwd(r, v, kk, w, kd, ka):
        vT = _to_feature_major(v)
        oT, st = _wkv_forward(r, vT, kk, w, kd, ka, n_ctx)
        return readout(oT), (r, vT, kk, w, kd, ka, st)

    def bwd(res, do):
        r, vT, kk, w, kd, ka, st = res
        doT = jnp.stack([_to_feature_major(do[0]), _to_feature_major(do[1])])
        dr, dw, dkd, dkn, dka, dvT = _wkv_backward(r, vT, kk, w, kd, ka, st, doT, n_ctx)
        return dr[0] + dr[1], _from_feature_major(dvT[0] + dvT[1]), -(dkn[0] + dkn[1]), dw, dkd, dka

    op.defvjp(fwd, bwd)
    return op


GD_HB = 2


def _dot(a, b, dims=((1,), (0,))):
    return lax.dot_general(a, b, (dims, ((), ())), precision=HI, preferred_element_type=F32)


def _gdn_chunk(S, q, k, v, g, beta, rev):
    R = q.shape[0]
    C = CHUNK
    nh = R // C
    ri = lax.broadcasted_iota(jnp.int32, (R, R), 0)
    cj = lax.broadcasted_iota(jnp.int32, (R, R), 1)
    same = lax.div(ri, C) == lax.div(cj, C)
    lag = (ri - cj) * (1 - 2 * rev.astype(jnp.int32))
    incl = same & (lag >= 0)
    strict = same & (lag > 0)
    eye = (ri == cj).astype(F32)
    gb = jnp.broadcast_to(g, (R, GD_N))
    bb = jnp.broadcast_to(beta, (R, GD_N))
    G = _dot(incl.astype(F32), gb)
    Gc = jnp.concatenate([G] * (R // GD_N), axis=1) if R > GD_N else G[:, :R]
    Grow = _dot(jnp.ones((R, R), F32), eye * Gc)
    decay = jnp.where(incl, jnp.exp(jnp.where(incl, Gc - Grow, 0.0)), 0.0)
    kb = k * bb
    A = jnp.where(strict, _dot(kb, k, ((1,), (1,))) * decay, 0.0)
    Nk = -A
    T = eye + Nk
    for _ in range(int(math.log2(C)) - 1):
        Nk = _dot(Nk, Nk)
        T = T + _dot(T, Nk)
    u = _dot(T, v * bb)
    w = _dot(T, kb * jnp.exp(G))
    attn = jnp.where(incl, _dot(q, k, ((1,), (1,))) * decay, 0.0)
    head = lax.div(lax.broadcasted_iota(jnp.int32, (R, GD_N), 0), C)

    def spread(a):
        return jnp.concatenate([jnp.where(head == h, a, 0.0) for h in range(nh)], axis=1)

    v_new = u - _dot(spread(w), S)
    o = _dot(spread(q * jnp.exp(G)), S) + _dot(attn, v_new)
    Gtot = _dot(same.astype(F32), gb)
    k_dec = k * jnp.exp(Gtot - G)
    Gs = jnp.concatenate([Gtot[h * C:(h + 1) * C] for h in range(nh) for _ in range(GD_N // C)], axis=0)
    S_new = S * jnp.exp(Gs) + _dot(spread(k_dec), v_new, ((0,), (0,)))
    return S_new, o


def _stack_heads(ref):
    return jnp.concatenate([ref[:, h * GD_N:(h + 1) * GD_N] for h in range(GD_HB)], axis=0)


def _unstack_heads(ref, a):
    for h in range(GD_HB):
        ref[:, h * GD_N:(h + 1) * GD_N] = a[h * CHUNK:(h + 1) * CHUNK]


def _gdn_specs(n, n_ctx, back):
    nc, ncc = n // CHUNK, n_ctx // CHUNK

    def ci(d, c):
        return _chunk_index(d, nc - 1 - c if back else c, ncc, nc)

    tok = pl.BlockSpec((CHUNK, GD_HB * GD_N), lambda d, h, c: (ci(d, c), h))
    tok_d = pl.BlockSpec((None, CHUNK, GD_HB * GD_N), lambda d, h, c: (d, ci(d, c), h))
    col = pl.BlockSpec((None, GD_HB, CHUNK, 1), lambda d, h, c: (d, h, ci(d, c), 0))
    st = pl.BlockSpec((None, GD_HB, None, GD_N, GD_N), lambda d, h, c: (d, h, ci(d, c), 0, 0))
    return nc, tok, tok_d, col, st


def _ride(ride, bcast, first, last, refs_in, refs_out, sems):
    if not ride:
        return
    start, wait = _exchange_copies(refs_in, refs_out, [bcast] * len(ride), *sems)
    pl.when(first)(start)
    pl.when(last)(wait)


def _grid_ends(nc):
    d, h, c = pl.program_id(0), pl.program_id(1), pl.program_id(2)
    first = (d == 0) & (h == 0) & (c == 0)
    last = (d == 1) & (h == GD_H // GD_HB - 1) & (c == nc - 1)
    return first, last


def _gdn_forward(q, k, v, g, beta, n_ctx, ride=()):
    n = q.shape[0]
    nc, tok, tok_d, col, st = _gdn_specs(n, n_ctx, False)
    nr = len(ride)

    def body(*refs):
        q_ref, k_ref, v_ref, g_ref, b_ref = refs[:5]
        o_ref, st_ref = refs[5 + nr:7 + nr]
        S_ref = refs[7 + 2 * nr]
        first, last = _grid_ends(nc)
        _ride(ride, True, first, last, refs[5:5 + nr], refs[7 + nr:7 + 2 * nr], refs[8 + 2 * nr:])

        @pl.when(pl.program_id(2) == 0)
        def _():
            S_ref[...] = jnp.zeros_like(S_ref)

        S = S_ref[...]
        st_ref[...] = S.reshape(GD_HB, GD_N, GD_N)
        S_new, o = _gdn_chunk(S, _stack_heads(q_ref), _stack_heads(k_ref), _stack_heads(v_ref),
                              g_ref[...].reshape(GD_HB * CHUNK, 1), b_ref[...].reshape(GD_HB * CHUNK, 1),
                              pl.program_id(0) == 1)
        S_ref[...] = S_new
        _unstack_heads(o_ref, o)

    any_spec = pl.BlockSpec(memory_space=pl.ANY)
    res = pl.pallas_call(
        body, grid=(2, GD_H // GD_HB, nc), in_specs=[tok, tok, tok, col, col] + [any_spec] * nr,
        out_specs=[tok_d, st] + [any_spec] * nr,
        out_shape=[jax.ShapeDtypeStruct((2, n, GD_W), F32), jax.ShapeDtypeStruct((2, GD_H, nc, GD_N, GD_N), F32)]
        + _exchange_out_shapes(ride, [True] * nr),
        scratch_shapes=[pltpu.VMEM((GD_HB * GD_N, GD_N), F32)] + (_exchange_sems(nr) if nr else []),
        compiler_params=_cparams(("arbitrary", "arbitrary", "arbitrary")), name="gdn_fwd")(q, k, v, g, beta, *ride)
    return res[0], res[1], tuple(res[2:])


def _gdn_backward(q, k, v, g, beta, st, do, n_ctx, ride=()):
    n = q.shape[0]
    nc, tok, tok_d, col, st_spec = _gdn_specs(n, n_ctx, True)
    nr = len(ride)

    def body(*refs):
        q_ref, k_ref, v_ref, g_ref, b_ref, st_ref, do_ref = refs[:7]
        dq_ref, dk_ref, dv_ref, dg_ref, db_ref = refs[7 + nr:12 + nr]
        dS_ref = refs[12 + 2 * nr]
        first, last = _grid_ends(nc)
        _ride(ride, False, first, last, refs[7:7 + nr], refs[12 + nr:12 + 2 * nr], refs[13 + 2 * nr:])

        @pl.when(pl.program_id(2) == 0)
        def _():
            dS_ref[...] = jnp.zeros_like(dS_ref)

        rev = pl.program_id(0) == 1
        rows = GD_HB * CHUNK
        _, vjp = jax.vjp(lambda S, q_, k_, v_, g_, b_: _gdn_chunk(S, q_, k_, v_, g_, b_, rev),
                         st_ref[...].reshape(GD_HB * GD_N, GD_N), _stack_heads(q_ref), _stack_heads(k_ref),
                         _stack_heads(v_ref), g_ref[...].reshape(rows, 1), b_ref[...].reshape(rows, 1))
        dS, dq, dk, dv, dg, db = vjp((dS_ref[...], _stack_heads(do_ref)))
        dS_ref[...] = dS
        _unstack_heads(dq_ref, dq)
        _unstack_heads(dk_ref, dk)
        _unstack_heads(dv_ref, dv)
        dg_ref[...] = dg.reshape(GD_HB, CHUNK, 1)
        db_ref[...] = db.reshape(GD_HB, CHUNK, 1)

    tok_out = jax.ShapeDtypeStruct((2, n, GD_W), F32)
    col_out = jax.ShapeDtypeStruct((2, GD_H, n, 1), F32)
    any_spec = pl.BlockSpec(memory_space=pl.ANY)
    res = pl.pallas_call(
        body, grid=(2, GD_H // GD_HB, nc), in_specs=[tok, tok, tok, col, col, st_spec, tok_d] + [any_spec] * nr,
        out_specs=[tok_d, tok_d, tok_d, col, col] + [any_spec] * nr,
        out_shape=[tok_out, tok_out, tok_out, col_out, col_out] + _exchange_out_shapes(ride, [False] * nr),
        scratch_shapes=[pltpu.VMEM((GD_HB * GD_N, GD_N), F32)] + (_exchange_sems(nr) if nr else []),
        compiler_params=_cparams(("arbitrary", "arbitrary", "arbitrary")), name="gdn_bwd")(
            q, k, v, g, beta, st, do, *ride)
    return tuple(res[:5]), tuple(res[5:])


def _gdn_op(n_ctx):
    @jax.custom_vjp
    def op(q, k, v, g, beta, shards, sinks):
        o, _, gathered = _gdn_forward(q, k, v, g, beta, n_ctx, shards)
        return o, gathered, sinks

    def fwd(q, k, v, g, beta, shards, sinks):
        o, st, gathered = _gdn_forward(q, k, v, g, beta, n_ctx, shards)
        return (o, gathered, sinks), (q, k, v, g, beta, st, shards)

    def bwd(res, cts):
        q, k, v, g, beta, st, shards = res
        do, _, dsinks = cts
        (dq, dk, dv, dg, db), received = _gdn_backward(q, k, v, g, beta, st, do, n_ctx, tuple(dsinks))
        return dq[0] + dq[1], dk[0] + dk[1], dv[0] + dv[1], dg, db, tuple(None for _ in shards), received

    op.defvjp(fwd, bwd)
    return op


def _bdot(a, b):
    return jnp.dot(a.astype(BF16), b.astype(BF16), preferred_element_type=F32)


def _rms(x, eps=NORM_EPS):
    return x * lax.rsqrt(jnp.mean(x * x, axis=-1, keepdims=True) + eps)


def _softplus(x):
    return jnp.maximum(x, 0.0) + jnp.log(1.0 + jnp.exp(-jnp.abs(x)))


def _heads(width, seg):
    e = (lax.div(lax.broadcasted_iota(jnp.int32, (width, 128), 0), seg)
         == lax.broadcasted_iota(jnp.int32, (width, 128), 1)).astype(F32)
    et = (lax.div(lax.broadcasted_iota(jnp.int32, (128, width), 1), seg)
          == lax.broadcasted_iota(jnp.int32, (128, width), 0)).astype(F32)
    return e, et


def _head_sum(x, seg):
    e, et = _heads(x.shape[1], seg)
    return _dot(_dot(x, e), et)


def _l2n(x, seg):
    return x * lax.rsqrt(jnp.maximum(_head_sum(x * x, seg), 1e-12))


def _f_norm_mod(x, g, sc, sh):
    return ((_rms(x) * g) * (1.0 + sc) + sh,)


def _f_rw_prep(pm, k_k, k_a, w0, wup, a0, aup, gup):
    k = pm[:, RW_W:2 * RW_W]
    kk = _l2n(k * k_k, RW_N)
    ws, kds, kas = [], [], []
    for d in range(2):
        wd = pm[:, 3 * RW_W + 128 * d:3 * RW_W + 128 * (d + 1)]
        ad = pm[:, 3 * RW_W + 256 + 128 * d:3 * RW_W + 256 + 128 * (d + 1)]
        wlog = -_softplus(-(w0[d:d + 1] + _bdot(jnp.tanh(wd), wup[128 * d:128 * (d + 1)]))) - 0.5
        ws.append(jnp.exp(-jnp.exp(wlog)))
        a = jax.nn.sigmoid(a0[d:d + 1] + _bdot(ad, aup[128 * d:128 * (d + 1)]))
        kds.append(k * (1.0 + (a - 1.0) * k_a))
        kas.append(kk * a)
    g = _bdot(jax.nn.sigmoid(pm[:, 3 * RW_W + 512:3 * RW_W + 512 + GATE_RANK]), gup)
    return kk, ws[0], ws[1], kds[0], kds[1], kas[0], kas[1], g


def _f_rw_read(o0, o1, r, kd0, kd1, v, g, r_k, gn_g, gn_b):
    o = o0 + o1
    c = o - _head_sum(o, RW_N) * (1.0 / RW_N)
    var = _head_sum(c * c, RW_N) * (1.0 / RW_N)
    on = c * lax.rsqrt(var + RW_GN_EPS) * gn_g + gn_b
    bonus = _head_sum(r * (kd0 + kd1) * r_k, RW_N) * v
    return ((on + bonus) * g,)


def _f_gd_prep(cq, ab, alog, dtb):
    qkv = cq * jax.nn.sigmoid(cq)
    q = _l2n(qkv[:, :GD_W], GD_N) * (GD_N ** -0.5)
    k = _l2n(qkv[:, GD_W:2 * GD_W], GD_N)
    v = qkv[:, 2 * GD_W:]
    glog = -jnp.exp(alog) * _softplus(ab + dtb)
    lane = lax.broadcasted_iota(jnp.int32, ab.shape, 1)
    return q, k, v, jnp.where(lane < 2 * GD_H, glog, jax.nn.sigmoid(ab))


def _f_gd_read(o0, o1, z, ng):
    o = o0 + o1
    y = o * lax.rsqrt(_head_sum(o * o, GD_N) * (1.0 / GD_N) + NORM_EPS) * jnp.concatenate([ng] * GD_H, axis=1)
    return (y * (z * jax.nn.sigmoid(z)),)


def _f_merge(za, zb, gates):
    return (jax.nn.sigmoid(gates[:, :D]) * za + jax.nn.sigmoid(gates[:, D:]) * zb,)


def _f_res_norm(x, att, g1, ng, sc, sh):
    x1 = x + g1 * att
    return x1, (_rms(x1) * ng) * (1.0 + sc) + sh


def _f_glu(gc, val):
    return (0.5 * gc * (1.0 + lax.erf(gc * (2.0 ** -0.5))) * val,)


def _f_final(x1, ff, target, g2, fg):
    y = _rms(x1 + g2 * ff) * fg
    err = y - target
    return (jnp.broadcast_to(0.5 * jnp.mean(err * err, axis=-1, keepdims=True), (x1.shape[0], 128)),)


def _exchange_copies(ins, outs, bcast, send_sems, recv_sems, local_sems):
    n = len(ins)
    x, y, c = lax.axis_index("x"), lax.axis_index("y"), lax.axis_index("c")
    me = 4 * x + 2 * y + c
    own = [pltpu.make_async_copy(ins[i] if bcast[i] else ins[i].at[me], outs[i].at[me], local_sems.at[i])
           for i in range(n)]
    sends, landings = [], []
    for k in range(1, NDEV):
        kx, ky, kc = (k >> 2) & 1, (k >> 1) & 1, k & 1
        px, py, pc = (1 - x if kx else x), (1 - y if ky else y), (1 - c if kc else c)
        peer = 4 * px + 2 * py + pc
        for i in range(n):
            sem = i * (NDEV - 1) + k - 1

            def copy(dst_block, i=i, sem=sem, peer=peer, dev=(px, py, pc)):
                return pltpu.make_async_remote_copy(
                    src_ref=ins[i] if bcast[i] else ins[i].at[peer], dst_ref=outs[i].at[dst_block],
                    send_sem=send_sems.at[sem], recv_sem=recv_sems.at[sem],
                    device_id=dev, device_id_type=pl.DeviceIdType.MESH)

            sends.append(copy(me))
            landings.append(copy(peer))

    def start():
        for cp in own + sends:
            cp.start()

    def wait():
        for cp, landing in zip(sends, landings):
            cp.wait_send()
            landing.wait_recv()
        for cp in own:
            cp.wait()

    return start, wait


def _exchange_sems(n):
    return [pltpu.SemaphoreType.DMA((n * (NDEV - 1),)), pltpu.SemaphoreType.DMA((n * (NDEV - 1),)),
            pltpu.SemaphoreType.DMA((n,))]


def _exchange_out_shapes(arrays, bcast):
    return [jax.ShapeDtypeStruct((NDEV,) + (a.shape if b else a.shape[1:]), a.dtype) for a, b in zip(arrays, bcast)]


def _exchange(arrays, bcast, name):
    n = len(arrays)

    def body(*refs):
        start, wait = _exchange_copies(refs[:n], refs[n:2 * n], bcast, *refs[2 * n:])
        start()
        wait()

    any_spec = pl.BlockSpec(memory_space=pl.ANY)
    return pl.pallas_call(
        body, in_specs=[any_spec] * n, out_specs=[any_spec] * n, out_shape=_exchange_out_shapes(arrays, bcast),
        scratch_shapes=_exchange_sems(n),
        compiler_params=pltpu.CompilerParams(has_side_effects=True), name=name)(*arrays)


def _adamw(slabs, w, m, v, name):
    R, C = w.shape
    ns = slabs.shape[0]
    tr = _pick(R, (256, 128, 64, 32, 16, 8))

    def body(s_ref, w_ref, m_ref, v_ref, g_ref, d_ref, nm_ref, nv_ref):
        g = s_ref[0].astype(F32)
        for i in range(1, ns):
            g = g + s_ref[i].astype(F32)
        nm = ADAM_B1 * m_ref[...] + (1.0 - ADAM_B1) * g
        nv = ADAM_B2 * v_ref[...] + (1.0 - ADAM_B2) * (g * g)
        m_hat = nm / (1.0 - ADAM_B1 ** ADAM_STEP)
        v_hat = nv / (1.0 - ADAM_B2 ** ADAM_STEP)
        g_ref[...] = g
        d_ref[...] = -ADAM_LR * (m_hat / (jnp.sqrt(v_hat) + ADAM_EPS) + ADAM_WD * w_ref[...])
        nm_ref[...] = nm
        nv_ref[...] = nv

    blk = pl.BlockSpec((tr, C), lambda i: (i, 0))
    out = jax.ShapeDtypeStruct((R, C), F32)
    return pl.pallas_call(
        body, grid=(R // tr,), in_specs=[pl.BlockSpec((ns, tr, C), lambda i: (0, i, 0)), blk, blk, blk],
        out_specs=[blk] * 4, out_shape=[out] * 4, compiler_params=_cparams(("parallel",)), name=name)(slabs, w, m, v)


def _silu(z):
    return z * jax.nn.sigmoid(z)


def _ada_forward(c_ext, w, b):
    n = w.shape[1]
    tn = _pick(n, (512, 256, 128))

    def body(c_ref, w_ref, b_ref, o_ref):
        o_ref[...] = _bdot(_silu(c_ref[...]), w_ref[...]) + b_ref[...]

    return pl.pallas_call(
        body, grid=(n // tn,),
        in_specs=[pl.BlockSpec(c_ext.shape, lambda j: (0, 0)), pl.BlockSpec((D, tn), lambda j: (0, j)),
                  pl.BlockSpec((1, tn), lambda j: (0, j))],
        out_specs=pl.BlockSpec((16, tn), lambda j: (0, j)), out_shape=jax.ShapeDtypeStruct((16, n), F32),
        compiler_params=_cparams(("parallel",)), name="ada_fwd")(c_ext, w, b)


def _ada_dmod(lat, ctxr):
    n = lat.shape[1]
    tn = 2048

    def body(lat_ref, ctx_ref, o_ref):
        o_ref[...] = jnp.concatenate([lat_ref[...], jnp.broadcast_to(jnp.sum(ctx_ref[...], axis=0, keepdims=True), (8, tn))], axis=0)

    blk = pl.BlockSpec((NDEV, tn), lambda j: (0, j))
    return pl.pallas_call(
        body, grid=(n // tn,), in_specs=[blk, blk],
        out_specs=pl.BlockSpec((16, tn), lambda j: (0, j)), out_shape=jax.ShapeDtypeStruct((16, n), F32),
        compiler_params=_cparams(("parallel",)), name="ada_dmod")(lat, ctxr)


def _f_colsum16(dm):
    return (jnp.broadcast_to(jnp.sum(dm, axis=0, keepdims=True), dm.shape),)


def _f_silu_grad(dpart, c_ext):
    sg = jax.nn.sigmoid(c_ext)
    return (dpart * sg * (1.0 + c_ext * (1.0 - sg)),)


def _f_silu(c_ext):
    return (_silu(c_ext),)


def _total_forward(rows):
    n = rows.shape[0]
    tm = _pick(n, (256, 128, 64, 32, 16, 8))

    def body(r_ref, o_ref):
        @pl.when(pl.program_id(0) == 0)
        def _():
            o_ref[...] = jnp.zeros_like(o_ref)

        o_ref[...] += jnp.broadcast_to(jnp.sum(r_ref[...], axis=0, keepdims=True), (8, 128))

    return pl.pallas_call(
        body, grid=(n // tm,), in_specs=[pl.BlockSpec((tm, 128), lambda i: (i, 0))],
        out_specs=pl.BlockSpec((8, 128), lambda i: (0, 0)), out_shape=jax.ShapeDtypeStruct((8, 128), F32),
        compiler_params=_cparams(("arbitrary",)), name="loss_total")(rows)


@jax.custom_vjp
def _total(rows):
    return _total_forward(rows)[0, 0]


def _total_fwd(rows):
    return _total_forward(rows)[0, 0], rows


def _total_bwd(rows, ct):
    lane = lax.broadcasted_iota(jnp.int32, rows.shape, 1)
    return (jnp.where(lane == 0, ct, 0.0).astype(F32),)


_total.defvjp(_total_fwd, _total_bwd)


def _pad_cols(a, width):
    return jnp.pad(a, ((0, 0), (0, width - a.shape[1])))


def _rw_cols(a):
    parts = [a[:, :3 * RW_W]]
    for i in range(4):
        parts.append(_pad_cols(a[:, 3 * RW_W + RANK * i:3 * RW_W + RANK * (i + 1)], 128))
    parts.append(a[:, 3 * RW_W + 4 * RANK:])
    return jnp.concatenate(parts, axis=1)


RW_COLS = 3 * RW_W + 4 * RANK + GATE_RANK
GD_COLS = 4 * GD_W + 4 * GD_H


def _split_w_in(w):
    gd = w[:, RW_COLS:RW_COLS + GD_COLS]
    rw = jnp.concatenate([_rw_cols(w[:, :RW_COLS]), _pad_cols(gd[:, 4 * GD_W:], 256)], axis=1)
    return w[:, RW_COLS + GD_COLS:], rw, gd[:, :4 * GD_W]


def _unshard(g, col):
    if col:
        return jnp.swapaxes(g, 0, 1).reshape(g.shape[1], NDEV * g.shape[2])
    return g.reshape(NDEV * g.shape[1], g.shape[2])


def _reshard(a, col):
    if col:
        return jnp.swapaxes(a.reshape(a.shape[0], NDEV, a.shape[1] // NDEV), 0, 1)
    return a.reshape(NDEV, a.shape[0] // NDEV, a.shape[1])


def _pack(parts, width, row_mult):
    flat = [p.reshape(-1) for p in parts]
    offs, o = [], 0
    for f in flat:
        offs.append(o)
        o += f.shape[0]
    rows = -(-o // (width * row_mult)) * row_mult
    cat = jnp.concatenate(flat + [jnp.zeros((rows * width - o,), flat[0].dtype)])
    return cat.reshape(rows, width), offs


BIG = ("w_in", "w_a_out", "w_b_out", "w_o", "ffn_w1", "ffn_w2")
LATE = BIG[1:]
SMALL = ("rw_w0", "rw_w_up", "rw_a0", "rw_a_up", "rw_g_up", "gd_conv_w", "ffn_conv_w")
ROW_SHARDED = ("w_o", "ffn_w2")
REPL = ("norm1_g", "norm2_g", "rw_mu", "rw_k_k", "rw_k_a", "rw_r_k", "rw_gn_g", "rw_gn_b", "gd_a_log", "gd_dt_bias",
        "gd_norm_g", "final_norm_g")
WEIGHTS = ('c_ctx', 'w_ada', 'b_ada', 'norm1_g', 'norm2_g', 'w_in', 'rw_mu', 'rw_k_k', 'rw_k_a', 'rw_r_k', 'rw_w0', 'rw_w_up', 'rw_a0', 'rw_a_up', 'rw_g_up', 'rw_gn_g', 'rw_gn_b', 'gd_conv_w', 'gd_a_log', 'gd_dt_bias', 'gd_norm_g', 'w_a_out', 'w_b_out', 'w_o', 'ffn_w1', 'ffn_conv_w', 'ffn_w2', 'final_norm_g')


def _view2d(a):
    return a.reshape(-1, a.shape[-1])


def _layer_loss(x, modl, modc, sinks, small, repl, ctx, target, big, n_ctx):
    sh1, sc1, g1, sh2, sc2, g2 = [modl[:, i * D:(i + 1) * D] for i in range(6)]
    csh1, csc1 = modc[:, :D], modc[:, D:2 * D]
    n1g, n2g, fg = repl["norm1_g"], repl["norm2_g"], repl["final_norm_g"].reshape(1, D)
    (h_lat,) = _rowwise(_f_norm_mod, "norm1_lat", tm=256)((x,), (n1g, sc1, sh1))
    (h_ctx,) = _rowwise(_f_norm_mod, "norm1_ctx", tm=256)((ctx,), (n1g, csc1, csh1))
    h = jnp.concatenate([h_ctx, h_lat], axis=0)
    wg, wr, wd = _split_w_in(big["w_in"])
    sg, sr, sd = _split_w_in(_unshard(sinks["w_in"], True))
    p_gate = _dense("proj_gate")(h_lat, wg, sg)
    p_rw = _dense("proj_rw")(h, wr, sr)
    p_gd = _dense("proj_gd")(h, wd, sd)
    n = h.shape[0]
    lat = slice(n_ctx, n)

    mu = _rw_cols(repl["rw_mu"])
    coef = jnp.concatenate([0.5 * mu, 1.0 - mu, 0.5 * mu], axis=0)
    pm = _stencil("rw_shift", ("seg", (n_ctx,)), ncols=RW_COLS + 128, tm=256, tc=768)(p_rw, coef)
    ab = p_rw[:, RW_COLS + 128:RW_COLS + 256]
    pad_rank = lambda a: jnp.pad(a, ((0, 0), (0, 128 - RANK), (0, 0))).reshape(256, RW_W)
    kk, w0, w1, kd0, kd1, ka0, ka1, g = _rowwise(_f_rw_prep, "rw_prep", tm=128)(
        (pm,), (repl["rw_k_k"], repl["rw_k_a"], small["rw_w0"], pad_rank(small["rw_w_up"]), small["rw_a0"],
                pad_rank(small["rw_a_up"]), small["rw_g_up"]))
    r, v = pm[:, :RW_W], pm[:, 2 * RW_W:3 * RW_W]
    o = _wkv_op(n_ctx)(r, v, kk, jnp.stack([w0, w1]), jnp.stack([kd0, kd1]), jnp.stack([ka0, ka1]))
    (ya,) = _rowwise(_f_rw_read, "rw_read", tm=256)(
        (o[0, lat], o[1, lat], r[lat], kd0[lat], kd1[lat], v[lat], g[lat]),
        (repl["rw_r_k"], repl["rw_gn_g"], repl["rw_gn_b"]))

    cq = _stencil("gd_conv", ("seg", (n_ctx,)), ncols=3 * GD_W, tm=256, tc=768)(p_gd, small["gd_conv_w"])
    z = p_gd[lat, 3 * GD_W:]
    lanes16 = lambda a: _pad_cols(a.reshape(1, 2 * GD_H), 128)
    q, k, v2, gb = _rowwise(_f_gd_prep, "gd_prep", tm=256)(
        (cq, ab), (lanes16(repl["gd_a_log"]), lanes16(repl["gd_dt_bias"])))
    per_head = lambda a: a.T.reshape(2, GD_H, n, 1)
    og, gathered, late_sinks = _gdn_op(n_ctx)(
        q, k, v2, per_head(gb[:, :2 * GD_H]), per_head(gb[:, 2 * GD_H:4 * GD_H]),
        tuple(big[k_] for k_ in LATE), tuple(sinks[k_] for k_ in LATE))
    (yb,) = _rowwise(_f_gd_read, "gd_read", tm=256)((og[0, lat], og[1, lat], z), (repl["gd_norm_g"],))
    w = {k_: _unshard(g_, k_ not in ROW_SHARDED) for k_, g_ in zip(LATE, gathered)}
    s = {k_: _unshard(s_, k_ not in ROW_SHARDED) for k_, s_ in zip(LATE, late_sinks)}

    za = _dense("a_out")(ya, w["w_a_out"], s["w_a_out"])
    zb = _dense("b_out")(yb, w["w_b_out"], s["w_b_out"])
    (merged,) = _rowwise(_f_merge, "merge", tm=256)((za, zb, p_gate), ())
    att = _dense("w_o")(merged, w["w_o"], s["w_o"])
    x1, h2 = _rowwise(_f_res_norm, "res_norm2", tm=256)((x, att), (g1, n2g, sc2, sh2))
    ug = _dense("ffn_gate")(h2, w["ffn_w1"][:, :D_FF], s["ffn_w1"][:, :D_FF])
    uv = _dense("ffn_val")(h2, w["ffn_w1"][:, D_FF:], s["ffn_w1"][:, D_FF:])
    gc = _stencil("ffn_conv", ("grid", GRID_W), ncols=D_FF, tm=256, tc=1408)(ug, small["ffn_conv_w"])
    (act,) = _rowwise(_f_glu, "glu", tm=64)((gc, uv), ())
    ff = _dense("ffn_w2")(act, w["ffn_w2"], s["ffn_w2"])
    (rows,) = _rowwise(_f_final, "final", tm=256, n_nondiff=1)((x1, ff, target), (g2, fg))
    return _total(rows)


def kernel(x, c, ctx, c_ctx, w_ada, b_ada, norm1_g, norm2_g, w_in, rw_mu, rw_k_k, rw_k_a, rw_r_k, rw_w0, rw_w_up, rw_a0, rw_a_up, rw_g_up, rw_gn_g, rw_gn_b, gd_conv_w, gd_a_log, gd_dt_bias, gd_norm_g, w_a_out, w_b_out, w_o, ffn_w1, ffn_conv_w, ffn_w2, final_norm_g, loss_target, m_c_ctx, m_w_ada, m_b_ada, m_norm1_g, m_norm2_g, m_w_in, m_rw_mu, m_rw_k_k, m_rw_k_a, m_rw_r_k, m_rw_w0, m_rw_w_up, m_rw_a0, m_rw_a_up, m_rw_g_up, m_rw_gn_g, m_rw_gn_b, m_gd_conv_w, m_gd_a_log, m_gd_dt_bias, m_gd_norm_g, m_w_a_out, m_w_b_out, m_w_o, m_ffn_w1, m_ffn_conv_w, m_ffn_w2, m_final_norm_g, v_c_ctx, v_w_ada, v_b_ada, v_norm1_g, v_norm2_g, v_w_in, v_rw_mu, v_rw_k_k, v_rw_k_a, v_rw_r_k, v_rw_w0, v_rw_w_up, v_rw_a0, v_rw_a_up, v_rw_g_up, v_rw_gn_g, v_rw_gn_b, v_gd_conv_w, v_gd_a_log, v_gd_dt_bias, v_gd_norm_g, v_w_a_out, v_w_b_out, v_w_o, v_ffn_w1, v_ffn_conv_w, v_ffn_w2, v_final_norm_g):
    args = dict(x=x, c=c, ctx=ctx, c_ctx=c_ctx, w_ada=w_ada, b_ada=b_ada, norm1_g=norm1_g, norm2_g=norm2_g, w_in=w_in, rw_mu=rw_mu, rw_k_k=rw_k_k, rw_k_a=rw_k_a, rw_r_k=rw_r_k, rw_w0=rw_w0, rw_w_up=rw_w_up, rw_a0=rw_a0, rw_a_up=rw_a_up, rw_g_up=rw_g_up, rw_gn_g=rw_gn_g, rw_gn_b=rw_gn_b, gd_conv_w=gd_conv_w, gd_a_log=gd_a_log, gd_dt_bias=gd_dt_bias, gd_norm_g=gd_norm_g, w_a_out=w_a_out, w_b_out=w_b_out, w_o=w_o, ffn_w1=ffn_w1, ffn_conv_w=ffn_conv_w, ffn_w2=ffn_w2, final_norm_g=final_norm_g, loss_target=loss_target, m_c_ctx=m_c_ctx, m_w_ada=m_w_ada, m_b_ada=m_b_ada, m_norm1_g=m_norm1_g, m_norm2_g=m_norm2_g, m_w_in=m_w_in, m_rw_mu=m_rw_mu, m_rw_k_k=m_rw_k_k, m_rw_k_a=m_rw_k_a, m_rw_r_k=m_rw_r_k, m_rw_w0=m_rw_w0, m_rw_w_up=m_rw_w_up, m_rw_a0=m_rw_a0, m_rw_a_up=m_rw_a_up, m_rw_g_up=m_rw_g_up, m_rw_gn_g=m_rw_gn_g, m_rw_gn_b=m_rw_gn_b, m_gd_conv_w=m_gd_conv_w, m_gd_a_log=m_gd_a_log, m_gd_dt_bias=m_gd_dt_bias, m_gd_norm_g=m_gd_norm_g, m_w_a_out=m_w_a_out, m_w_b_out=m_w_b_out, m_w_o=m_w_o, m_ffn_w1=m_ffn_w1, m_ffn_conv_w=m_ffn_conv_w, m_ffn_w2=m_ffn_w2, m_final_norm_g=m_final_norm_g, v_c_ctx=v_c_ctx, v_w_ada=v_w_ada, v_b_ada=v_b_ada, v_norm1_g=v_norm1_g, v_norm2_g=v_norm2_g, v_w_in=v_w_in, v_rw_mu=v_rw_mu, v_rw_k_k=v_rw_k_k, v_rw_k_a=v_rw_k_a, v_rw_r_k=v_rw_r_k, v_rw_w0=v_rw_w0, v_rw_w_up=v_rw_w_up, v_rw_a0=v_rw_a0, v_rw_a_up=v_rw_a_up, v_rw_g_up=v_rw_g_up, v_rw_gn_g=v_rw_gn_g, v_rw_gn_b=v_rw_gn_b, v_gd_conv_w=v_gd_conv_w, v_gd_a_log=v_gd_a_log, v_gd_dt_bias=v_gd_dt_bias, v_gd_norm_g=v_gd_norm_g, v_w_a_out=v_w_a_out, v_w_b_out=v_w_b_out, v_w_o=v_w_o, v_ffn_w1=v_ffn_w1, v_ffn_conv_w=v_ffn_conv_w, v_ffn_w2=v_ffn_w2, v_final_norm_g=v_final_norm_g)
    me = 4 * lax.axis_index("x") + 2 * lax.axis_index("y") + lax.axis_index("c")
    x2, ctx2, target = args["x"][0], args["ctx"][0], args["loss_target"][0]
    shard = {k: _view2d(args[k][0]) for k in BIG + SMALL}

    small_pack, small_offs = _pack([args["c"]] + [shard[k] for k in SMALL], 128, 8)
    gathered = _exchange([shard["w_in"].astype(BF16), small_pack], [True, True], "gather_weights")
    small_all = gathered[-1].reshape(NDEV, -1)

    def unpack(allg, off, k):
        r_, c_ = shard[k].shape
        return _unshard(allg[:, off:off + r_ * c_].reshape(NDEV, r_, c_), k not in ROW_SHARDED)

    big = {k: shard[k].astype(BF16) for k in LATE}
    big["w_in"] = _unshard(gathered[0], True)
    small = {k: unpack(small_all, o, k) for k, o in zip(SMALL, small_offs[1:])}
    small["rw_w_up"] = small["rw_w_up"].reshape(2, RANK, RW_W)
    small["rw_a_up"] = small["rw_a_up"].reshape(2, RANK, RW_W)
    c_all = small_all[:, :D]

    c_ext = jnp.concatenate([c_all, args["c_ctx"].reshape(1, D), jnp.zeros((7, D), F32)], axis=0)
    w_ada = args["w_ada"][0]
    nb = w_ada.shape[1]
    mod_blk = _ada_forward(c_ext, w_ada, lax.dynamic_slice(args["b_ada"], (0, me * nb), (1, nb)))
    (mod_all,) = _exchange([mod_blk], [True], "gather_mod")
    mod_all = jnp.swapaxes(mod_all, 0, 1).reshape(16, NDEV * nb)
    modl = lax.dynamic_slice(mod_all, (me, 0), (1, NDEV * nb))
    modc = mod_all[NDEV:NDEV + 1]

    repl = {k: args[k] for k in REPL}
    sinks = {k: jnp.zeros((NDEV,) + shard[k].shape, GRAD_WIRE) for k in BIG}
    n_ctx = ctx2.shape[0]
    loss, grads = jax.value_and_grad(_layer_loss, argnums=(0, 1, 2, 3, 4, 5))(
        x2, modl, modc, sinks, small, repl, ctx2, target, big, n_ctx)
    dx, dmodl, dmodc, dbig, dsmall, drepl = grads
    loss = lax.psum(loss, ("x", "y", "c"))

    rep_pack, rep_offs = _pack([dmodl, dmodc] + [drepl[k] for k in REPL], 128, 8)
    (rep_all,) = _exchange([rep_pack], [True], "gather_small_grads")
    rep_all = rep_all.reshape(NDEV, -1)
    six_d = NDEV * nb
    dmod_ext = _ada_dmod(rep_all[:, :six_d], rep_all[:, six_d:2 * six_d])
    dmod_blk = lax.dynamic_slice(dmod_ext, (0, me * nb), (16, nb))
    (s_ext,) = _rowwise(_f_silu, "ada_silu", tm=16)((c_ext,), ())
    g_w_ada = _mm(s_ext, dmod_blk, ta=True, name="ada_dw")
    dpart = _mm(dmod_blk, w_ada, tb=True, name="ada_dc")
    (dpart,) = _rowwise(_f_silu_grad, "ada_dsilu", tm=16)((dpart, c_ext), ())

    flat_small = [_reshard(_view2d(dsmall[k]) if k != "ffn_conv_w" else dsmall[k], True).reshape(NDEV, -1) for k in SMALL]
    small_send = jnp.concatenate(flat_small, axis=1)
    pad = -small_send.shape[1] % 1024
    small_send = jnp.pad(small_send, ((0, 0), (0, pad))).reshape(NDEV, -1, 128)
    got = _exchange([dbig["w_in"], small_send, dpart[NDEV:NDEV + 1]], [False, False, True], "scatter_grads")
    small_got = got[1].reshape(NDEV, -1)

    slabs = {k: dbig[k] for k in LATE}
    slabs["w_in"] = got[0]
    o = 0
    for k in SMALL:
        r_, c_ = shard[k].shape
        slabs[k] = small_got[:, o:o + r_ * c_].reshape(NDEV, r_, c_)
        o += r_ * c_
    slabs["w_ada"] = g_w_ada[None]
    slabs["c_ctx"] = got[2]
    slabs["b_ada"] = jnp.concatenate([rep_all[:, :six_d], rep_all[:, six_d:2 * six_d]], axis=0)[:, None, :]
    for k, off in zip(REPL, rep_offs[2:]):
        size = args[k].size
        slabs[k] = rep_all[:, off:off + size].reshape((NDEV,) + _view2d(args[k]).shape)

    outs = {}
    for k in WEIGHTS:
        shape = args[k].shape
        res = _adamw(slabs[k], _view2d(args[k]), _view2d(args["m_" + k]), _view2d(args["v_" + k]), "adamw_" + k)
        outs[k] = [a.reshape(shape) for a in res]
    return (loss, dx[None]) + tuple(outs[k][i] for i in range(4) for k in WEIGHTS)
```
